```python
import jax, jax.numpy as jnp
from jax import lax
import numpy as np

D_MODEL = 1024
BATCH = 8
SEQ = 8192
DEPTH = 1
DEC_BATCH = 8
DEC_SEQ = 32
PAST_LEN = 2048

CHUNK = 64
EPS = 1e-6
NEG_INF = -1e30
MLA_HEADS = 8
QK_NOPE = 64
QK_ROPE = 32
V_DIM = 64
Q_RANK = 384
KV_RANK = 256
MLA_WIDTH = MLA_HEADS * V_DIM
ROPE_THETA = 10000.0
Q_BLOCK = 128
CONV_WIDTH = 512
CONV_K = 3
D_MIX = MLA_WIDTH + CONV_WIDTH
SPLITS = (Q_RANK, Q_RANK + KV_RANK, Q_RANK + KV_RANK + QK_ROPE,
          Q_RANK + KV_RANK + QK_ROPE + CONV_WIDTH,
          Q_RANK + KV_RANK + QK_ROPE + 2 * CONV_WIDTH)
D_IN = Q_RANK + KV_RANK + QK_ROPE + 3 * CONV_WIDTH
PEER_HEADS = 8
N_KEYS = 128
N_EXPERTS = N_KEYS * N_KEYS
PEER_TOPK = 16
PEER_DQ = 256
PEER_BLOCK = 128

kernel_name = "hymba_mla_shortconv_peer_stream_step"


def rmsnorm(x, g):
    xf = x.astype(jnp.float32)
    y = xf * lax.rsqrt(jnp.mean(xf * xf, axis=-1, keepdims=True) + EPS)
    return (y * g.astype(jnp.float32)).astype(x.dtype)


def rope_tables(pos):
    half = QK_ROPE // 2
    freqs = ROPE_THETA ** (-jnp.arange(half, dtype=jnp.float32) / half)
    ang = pos.astype(jnp.float32)[:, None] * freqs[None, :]
    return jnp.cos(ang), jnp.sin(ang)


def apply_rope(x, cos, sin):
    x1, x2 = jnp.split(x, 2, axis=-1)
    c = cos.astype(x.dtype)
    s = sin.astype(x.dtype)
    return jnp.concatenate([x1 * c - x2 * s, x1 * s + x2 * c], axis=-1)


def chunk_causal_attention(q, k, v, q_pos, k_pos):
    B, Sq, H, Dqk = q.shape
    qb = Q_BLOCK if Sq % Q_BLOCK == 0 else Sq
    nb = Sq // qb
    scale = Dqk ** -0.5
    k_chunk = k_pos // CHUNK
    q_blocks = q.reshape(B, nb, qb, H, Dqk).transpose(1, 0, 2, 3, 4)
    p_blocks = q_pos.reshape(nb, qb)

    def block(args):
        qblk, pblk = args
        s = jnp.einsum('bqhd,bkhd->bhqk', qblk, k,
                       preferred_element_type=jnp.float32) * scale
        mask = k_chunk[None, :] <= (pblk // CHUNK)[:, None]
        s = jnp.where(mask[None, None], s, NEG_INF)
        p = jax.nn.softmax(s, axis=-1)
        return jnp.einsum('bhqk,bkhd->bqhd', p.astype(v.dtype), v)

    out = lax.map(block, (q_blocks, p_blocks))
    return out.transpose(1, 0, 2, 3, 4).reshape(B, Sq, H, v.shape[-1])


def peer_ffn(h, w_peer_q, peer_keys, peer_u, peer_v):
    B, S, D = h.shape
    T = B * S
    n_blk = -(-T // PEER_BLOCK)
    flat = jnp.pad(h.reshape(T, D), ((0, n_blk * PEER_BLOCK - T), (0, 0)))

    def block(xb):
        q = (xb @ w_peer_q).reshape(PEER_BLOCK, PEER_HEADS, 2, PEER_DQ // 2)
        s = jnp.einsum('thcd,hcnd->thcn', q, peer_keys,
                       preferred_element_type=jnp.float32)
        s_top, i_top = lax.top_k(s, PEER_TOPK)
        cand = (s_top[:, :, 0, :, None] + s_top[:, :, 1, None, :]).reshape(
            PEER_BLOCK, PEER_HEADS, PEER_TOPK * PEER_TOPK)
        cand_idx = (i_top[:, :, 0, :, None] * N_KEYS + i_top[:, :, 1, None, :]).reshape(
            PEER_BLOCK, PEER_HEADS, PEER_TOPK * PEER_TOPK)
        best, sel = lax.top_k(cand, PEER_TOPK)
        experts = jnp.take_along_axis(cand_idx, sel, axis=-1)
        g = jax.nn.softmax(best, axis=-1)
        u = peer_u[experts]
        a = jax.nn.gelu(jnp.einsum('thkd,td->thk', u, xb), approximate=False)
        v = peer_v[experts]
        return jnp.einsum('thk,thkd->td', (g * a).astype(v.dtype), v)

    out = lax.map(block, flat.reshape(n_blk, PEER_BLOCK, D))
    return out.reshape(n_blk * PEER_BLOCK, D)[:T].reshape(B, S, D)


def layer(x, pos, past_kv, past_kpe, past_conv,
          g_attn_norm, w_in, g_q, w_uq, g_kv, w_ukv, w_conv,
          g_mla_out, g_conv_out, w_out, g_ffn_norm,
          w_peer_q, peer_keys, peer_u, peer_v):
    B, S, _ = x.shape
    h = rmsnorm(x, g_attn_norm)
    z = h @ w_in
    c_q, c_kv, k_pe, u_conv, b_gate, c_gate = jnp.split(z, SPLITS, axis=-1)
    cos, sin = rope_tables(pos)

    q = (rmsnorm(c_q, g_q) @ w_uq).reshape(B, S, MLA_HEADS, QK_NOPE + QK_ROPE)
    q_nope, q_pe = jnp.split(q, [QK_NOPE], axis=-1)
    q_pe = apply_rope(q_pe, cos[:, None, :], sin[:, None, :])
    c_kv = rmsnorm(c_kv, g_kv)
    k_pe = apply_rope(k_pe, cos, sin)
    if past_kv is None:
        kv_all, kpe_all, k_pos = c_kv, k_pe, pos
    else:
        kv_all = jnp.concatenate([past_kv, c_kv], axis=1)
        kpe_all = jnp.concatenate([past_kpe, k_pe], axis=1)
        k_pos = jnp.concatenate([jnp.arange(past_kv.shape[1], dtype=pos.dtype), pos])
    Sk = kv_all.shape[1]
    kv = (kv_all @ w_ukv).reshape(B, Sk, MLA_HEADS, QK_NOPE + V_DIM)
    k_nope, v = jnp.split(kv, [QK_NOPE], axis=-1)
    k = jnp.concatenate(
        [k_nope, jnp.broadcast_to(kpe_all[:, :, None, :], (B, Sk, MLA_HEADS, QK_ROPE))], axis=-1)
    attn = chunk_causal_attention(jnp.concatenate([q_nope, q_pe], axis=-1), k, v, pos, k_pos)
    attn = attn.reshape(B, S, MLA_WIDTH)

    u = c_gate * u_conv
    if past_conv is None:
        past_conv = jnp.zeros((B, CONV_K - 1, CONV_WIDTH), u.dtype)
    up = jnp.concatenate([past_conv, u], axis=1)
    conv = up[:, 0:S] * w_conv[0]
    for j in range(1, CONV_K):
        conv = conv + up[:, j:j + S] * w_conv[j]
    conv_out = b_gate * conv
    new_conv = up[:, -(CONV_K - 1):]

    mix = jnp.concatenate([rmsnorm(attn, g_mla_out), rmsnorm(conv_out, g_conv_out)], axis=-1)
    x = x + mix @ w_out
    x = x + peer_ffn(rmsnorm(x, g_ffn_norm), w_peer_q, peer_keys, peer_u, peer_v)
    return x, c_kv, k_pe, new_conv


def setup_inputs(seed: int = 0) -> dict:
    key = jax.random.key(seed)
    ks = jax.random.split(key, 24)
    nrm = lambda k, shape, s: jax.random.normal(k, shape, jnp.float32) * s
    gain = lambda k, shape: 1.0 + 0.02 * jax.random.normal(k, shape, jnp.float32)
    return {
        "x_prompt": nrm(ks[0], (BATCH, SEQ, D_MODEL), 1.0),
        "x_sample": nrm(ks[1], (DEC_BATCH, DEC_SEQ, D_MODEL), 1.0),
        "cache_kv_latent": nrm(ks[2], (DEPTH, DEC_BATCH, PAST_LEN, KV_RANK), 1.0),
        "cache_k_rope": nrm(ks[3], (DEPTH, DEC_BATCH, PAST_LEN, QK_ROPE), 1.0),
        "state_conv": nrm(ks[4], (DEPTH, DEC_BATCH, CONV_K - 1, CONV_WIDTH), 1.0),
        "g_attn_norm": gain(ks[5], (DEPTH, D_MODEL)),
        "w_in": nrm(ks[6], (DEPTH, D_MODEL, D_IN), D_MODEL ** -0.5),
        "g_q": gain(ks[7], (DEPTH, Q_RANK)),
        "w_uq": nrm(ks[8], (DEPTH, Q_RANK, MLA_HEADS * (QK_NOPE + QK_ROPE)), Q_RANK ** -0.5),
        "g_kv": gain(ks[9], (DEPTH, KV_RANK)),
        "w_ukv": nrm(ks[10], (DEPTH, KV_RANK, MLA_HEADS * (QK_NOPE + V_DIM)), KV_RANK ** -0.5),
        "w_conv": nrm(ks[11], (DEPTH, CONV_K, CONV_WIDTH), CONV_K ** -0.5),
        "g_mla_out": gain(ks[12], (DEPTH, MLA_WIDTH)),
        "g_conv_out": gain(ks[13], (DEPTH, CONV_WIDTH)),
        "w_out": nrm(ks[14], (DEPTH, D_MIX, D_MODEL), D_MIX ** -0.5),
        "g_ffn_norm": gain(ks[15], (DEPTH, D_MODEL)),
        "w_peer_q": nrm(ks[16], (DEPTH, D_MODEL, PEER_HEADS * PEER_DQ), D_MODEL ** -0.5),
        "peer_keys": nrm(ks[17], (DEPTH, PEER_HEADS, 2, N_KEYS, PEER_DQ // 2), (PEER_DQ // 2) ** -0.5),
        "peer_u": nrm(ks[18], (DEPTH, N_EXPERTS, D_MODEL), D_MODEL ** -0.5),
        "peer_v": nrm(ks[19], (DEPTH, N_EXPERTS, D_MODEL), PEER_HEADS ** -0.5),
        "g_final": gain(ks[20], (D_MODEL,)),
    }


def reference(x_prompt, x_sample, cache_kv_latent, cache_k_rope, state_conv,
              g_attn_norm, w_in, g_q, w_uq, g_kv, w_ukv, w_conv,
              g_mla_out, g_conv_out, w_out, g_ffn_norm,
              w_peer_q, peer_keys, peer_u, peer_v, g_final):
    pos_p = jnp.arange(x_prompt.shape[1], dtype=jnp.int32)
    pos_s = cache_kv_latent.shape[2] + jnp.arange(x_sample.shape[1], dtype=jnp.int32)
    hp, hs = x_prompt, x_sample
    kv_p, kpe_p, conv_p, kv_s, kpe_s, conv_s = [], [], [], [], [], []
    for l in range(DEPTH):
        params = (g_attn_norm[l], w_in[l], g_q[l], w_uq[l], g_kv[l], w_ukv[l], w_conv[l],
                  g_mla_out[l], g_conv_out[l], w_out[l], g_ffn_norm[l],
                  w_peer_q[l], peer_keys[l], peer_u[l], peer_v[l])
        hp, a, b, c = layer(hp, pos_p, None, None, None, *params)
        kv_p.append(a); kpe_p.append(b); conv_p.append(c)
        hs, a, b, c = layer(hs, pos_s, cache_kv_latent[l], cache_k_rope[l], state_conv[l], *params)
        kv_s.append(a); kpe_s.append(b); conv_s.append(c)
    y_prompt = rmsnorm(hp, g_final)
    y_sample = rmsnorm(hs, g_final)
    return (y_prompt, y_sample,
            jnp.stack(kv_p), jnp.stack(kpe_p), jnp.stack(conv_p),
            jnp.stack(kv_s), jnp.stack(kpe_s), jnp.stack(conv_s))
```

```python
import functools

import numpy as np
import jax
import jax.numpy as jnp
from jax import lax
from jax.experimental import pallas as pl
from jax.experimental.pallas import tpu as pltpu

F32 = jnp.float32
BF16 = jnp.bfloat16
I32 = jnp.int32

EPS = 1e-6
NEG_INF = -1e30
CHUNK_SHIFT = 6
HEADS = 8
QK_NOPE = 64
QK_ROPE = 32
V_DIM = 64
Q_RANK = 384
KV_RANK = 256
CONV_WIDTH = 512
ROPE_THETA = 10000.0
HEAD_PAD = 128
ROPE_LO = QK_NOPE
ROPE_HALF = QK_ROPE // 2
PEER_HEADS = 8
N_KEYS = 128
PEER_TOPK = 16
LANES = 128
SUBLANES = 8
W_PITCH = 136

CQ0, CKV0, KPE0, UC0, BG0, CG0, ZW = 0, 384, 640, 768, 1280, 1792, 2304

CANDS = sorted(
    [(a, b) for a in range(PEER_TOPK) for b in range(PEER_TOPK) if (a + 1) * (b + 1) <= PEER_TOPK],
    key=lambda ab: ab[0] * PEER_TOPK + ab[1])
N_CAND = len(CANDS)
N_CAND_PAD = -(-N_CAND // SUBLANES) * SUBLANES


def _rms(x, g):
    ms = jnp.mean(x * x, axis=-1, keepdims=True)
    return x * lax.rsqrt(ms + EPS) * g


def _swap_rope_halves(v):
    lane = lax.broadcasted_iota(I32, v.shape, 1)
    return jnp.where(lane < ROPE_LO + ROPE_HALF,
                     pltpu.roll(v, LANES - ROPE_HALF, 1), pltpu.roll(v, ROPE_HALF, 1))


def _params(vmem_mb, n_axes):
    return pltpu.CompilerParams(dimension_semantics=("arbitrary",) * n_axes,
                                vmem_limit_bytes=vmem_mb * 1024 * 1024)


def _front_kernel(x_ref, win_ref, gattn_ref, gq_ref, wuq_ref, gkv_ref, wconv_ref, gconv_ref,
                  cq_ref, sq_ref, ck_ref, sk_ref, past_ref,
                  q_ref, ckv_ref, kpe_ref, convn_ref, newconv_ref, carry_ref,
                  *, ts, nc_tile, nc_row):
    si = pl.program_id(1)
    x = x_ref[0]
    h = _rms(x, gattn_ref[...]).astype(BF16)
    z = jnp.dot(h, win_ref[...], preferred_element_type=F32)

    cqn = _rms(z[:, CQ0:CKV0], gq_ref[...]).astype(BF16)
    q = jnp.dot(cqn, wuq_ref[...], preferred_element_type=F32)
    cq = cq_ref[...]
    sq = sq_ref[...]
    for hd in range(HEADS):
        qh = q[:, hd * HEAD_PAD:(hd + 1) * HEAD_PAD]
        q_ref[0, hd] = (qh * cq + _swap_rope_halves(qh) * sq).astype(BF16)

    ckv_ref[0] = _rms(z[:, CKV0:KPE0], gkv_ref[...])

    kp = z[:, KPE0:UC0]
    kp = kp * ck_ref[...] + _swap_rope_halves(kp) * sk_ref[...]
    kpe_ref[0] = kp[:, ROPE_LO:ROPE_LO + QK_ROPE]

    u = z[:, CG0:ZW] * z[:, UC0:BG0]

    @pl.when(si == 0)
    def _():
        carry_ref[0:2, :] = past_ref[0]

    prev2 = carry_ref[0:1, :]
    prev1 = carry_ref[1:2, :]
    row = lax.broadcasted_iota(I32, u.shape, 0)
    u1 = jnp.where(row == 0, prev1, pltpu.roll(u, 1, 0))
    u2 = jnp.where(row == 0, prev2, jnp.where(row == 1, prev1, pltpu.roll(u, 2, 0)))
    conv = u2 * wconv_ref[0:1, :] + u1 * wconv_ref[1:2, :] + u * wconv_ref[2:3, :]
    convn_ref[0] = _rms(z[:, BG0:CG0] * conv, gconv_ref[...]).astype(BF16)
    carry_ref[0:2, :] = u[ts - 2:ts, :]

    @pl.when(si == nc_tile)
    def _():
        newconv_ref[0] = u[nc_row:nc_row + 2, :]


def _front(x, past_conv, tabs, w, *, ts, s_valid):
    b, s, d = x.shape
    cq, sq, ck, sk = tabs
    full = lambda a: pl.BlockSpec(a.shape, lambda i, j: (0,) * a.ndim)
    tab = pl.BlockSpec((ts, LANES), lambda i, j: (j, 0))
    kern = functools.partial(_front_kernel, ts=ts, nc_tile=(s_valid - 2) // ts,
                             nc_row=(s_valid - 2) % ts)
    return pl.pallas_call(
        kern,
        grid=(b, s // ts),
        in_specs=[pl.BlockSpec((1, ts, d), lambda i, j: (i, j, 0)),
                  full(w["w_in"]), full(w["g_attn"]), full(w["g_q"]), full(w["w_uq"]),
                  full(w["g_kv"]), full(w["w_conv"]), full(w["g_conv"]),
                  tab, tab, tab, tab,
                  pl.BlockSpec((1, 2, CONV_WIDTH), lambda i, j: (i, 0, 0))],
        out_specs=[pl.BlockSpec((1, HEADS, ts, HEAD_PAD), lambda i, j: (i, 0, j, 0)),
                   pl.BlockSpec((1, ts, KV_RANK), lambda i, j: (i, j, 0)),
                   pl.BlockSpec((1, ts, QK_ROPE), lambda i, j: (i, j, 0)),
                   pl.BlockSpec((1, ts, CONV_WIDTH), lambda i, j: (i, j, 0)),
                   pl.BlockSpec((1, 2, CONV_WIDTH), lambda i, j: (i, 0, 0))],
        out_shape=[jax.ShapeDtypeStruct((b, HEADS, s, HEAD_PAD), BF16),
                   jax.ShapeDtypeStruct((b, s, KV_RANK), F32),
                   jax.ShapeDtypeStruct((b, s, QK_ROPE), F32),
                   jax.ShapeDtypeStruct((b, s, CONV_WIDTH), BF16),
                   jax.ShapeDtypeStruct((b, 2, CONV_WIDTH), F32)],
        scratch_shapes=[pltpu.VMEM((SUBLANES, CONV_WIDTH), F32)],
        compiler_params=_params(48, 2),
        name="front",
    )(x, w["w_in"], w["g_attn"], w["g_q"], w["w_uq"], w["g_kv"], w["w_conv"], w["g_conv"],
      cq, sq, ck, sk, past_conv)


def _kvup_kernel(ckv_ref, kpe_ref, wk_ref, wvt_ref, place_ref, k_ref, vt_ref):
    c = ckv_ref[0].astype(BF16)
    kn = jnp.dot(c, wk_ref[...], preferred_element_type=F32)
    kp = jnp.dot(kpe_ref[0].astype(BF16), place_ref[...], preferred_element_type=F32)
    for hd in range(HEADS):
        k_ref[0, hd] = (kn[:, hd * HEAD_PAD:(hd + 1) * HEAD_PAD] + kp).astype(BF16)
    vt = lax.dot_general(wvt_ref[...], c, (((1,), (1,)), ((), ())),
                         preferred_element_type=F32)
    for hd in range(HEADS):
        vt_ref[0, hd, 0] = vt[hd * V_DIM:(hd + 1) * V_DIM, :].astype(BF16)


def _kv_up(ckv, kpe, w, *, tk):
    b, sk, _ = ckv.shape
    nkb = sk // tk
    full = lambda a: pl.BlockSpec(a.shape, lambda i, j: (0,) * a.ndim)
    return pl.pallas_call(
        _kvup_kernel,
        grid=(b, nkb),
        in_specs=[pl.BlockSpec((1, tk, KV_RANK), lambda i, j: (i, j, 0)),
                  pl.BlockSpec((1, tk, QK_ROPE), lambda i, j: (i, j, 0)),
                  full(w["w_k"]), full(w["w_vt"]), full(w["place"])],
        out_specs=[pl.BlockSpec((1, HEADS, tk, HEAD_PAD), lambda i, j: (i, 0, j, 0)),
                   pl.BlockSpec((1, HEADS, 1, V_DIM, tk), lambda i, j: (i, 0, j, 0, 0))],
        out_shape=[jax.ShapeDtypeStruct((b, HEADS, sk, HEAD_PAD), BF16),
                   jax.ShapeDtypeStruct((b, HEADS, nkb, V_DIM, tk), BF16)],
        compiler_params=_params(32, 2),
        name="kv_up",
    )(ckv, kpe, w["w_k"], w["w_vt"], w["place"])


def _attn_kernel(q_ref, k_ref, vt_ref, o_ref, *, tq, tk, nkb, q_off, sk_valid):
    qi = pl.program_id(2)
    q = q_ref[0, 0]
    q0 = q_off + qi * tq
    qch = lax.shift_right_logical(q0 + lax.broadcasted_iota(I32, (1, tq), 1), CHUNK_SHIFT)
    last_visible = lax.shift_left(lax.shift_right_logical(q0 + tq - 1, CHUNK_SHIFT) + 1,
                                  CHUNK_SHIFT) - 1
    nb = jnp.minimum(nkb, last_visible // tk + 1)

    def body(kb, carry):
        m, l, acc = carry
        k = k_ref[0, 0, pl.ds(pl.multiple_of(kb * tk, tk), tk), :]
        s = lax.dot_general(k, q, (((1,), (1,)), ((), ())), preferred_element_type=F32)
        kpos = kb * tk + lax.broadcasted_iota(I32, (tk, 1), 0)
        vis = (lax.shift_right_logical(kpos, CHUNK_SHIFT) <= qch) & (kpos < sk_valid)
        s = jnp.where(vis, s, NEG_INF)
        m_new = jnp.maximum(m, jnp.max(s, axis=0, keepdims=True))
        p = jnp.exp(s - m_new)
        alpha = jnp.exp(m - m_new)
        l = alpha * l + jnp.sum(p, axis=0, keepdims=True)
        acc = alpha * acc + jnp.dot(vt_ref[0, 0, kb], p.astype(BF16),
                                    preferred_element_type=F32)
        return m_new, l, acc

    init = (jnp.full((1, tq), NEG_INF, F32), jnp.zeros((1, tq), F32), jnp.zeros((V_DIM, tq), F32))
    _, l, acc = lax.fori_loop(0, nb, body, init)
    o_ref[0, 0] = acc / l


def _attention(q, k, vt, *, tq, q_off, sk_valid):
    b, _, s, _ = q.shape
    _, _, nkb, _, tk = vt.shape
    sk = k.shape[2]
    kern = functools.partial(_attn_kernel, tq=tq, tk=tk, nkb=nkb, q_off=q_off, sk_valid=sk_valid)
    return pl.pallas_call(
        kern,
        grid=(b, HEADS, s // tq),
        in_specs=[pl.BlockSpec((1, 1, tq, HEAD_PAD), lambda i, h, j: (i, h, j, 0)),
                  pl.BlockSpec((1, 1, sk, HEAD_PAD), lambda i, h, j: (i, h, 0, 0)),
                  pl.BlockSpec((1, 1, nkb, V_DIM, tk), lambda i, h, j: (i, h, 0, 0, 0))],
        out_specs=pl.BlockSpec((1, 1, V_DIM, tq), lambda i, h, j: (i, h, 0, j)),
        out_shape=jax.ShapeDtypeStruct((b, HEADS, V_DIM, s), F32),
        compiler_params=_params(32, 3),
        name="attn",
    )(q, k, vt)


def _merge_kernel(x_ref, at_ref, cn_ref, gm_ref, woa_ref, woc_ref, gf_ref, x1_ref, hn_ref, *, ts):
    a = at_ref[0].reshape(HEADS * V_DIM, ts).T
    an = _rms(a, gm_ref[...]).astype(BF16)
    mix = (jnp.dot(an, woa_ref[...], preferred_element_type=F32)
           + jnp.dot(cn_ref[0], woc_ref[...], preferred_element_type=F32))
    x1 = x_ref[0] + mix
    x1_ref[0] = x1
    hn_ref[0] = _rms(x1, gf_ref[...]).astype(BF16)


def _merge(x, attn_t, convn, w, *, ts):
    b, s, d = x.shape
    full = lambda a: pl.BlockSpec(a.shape, lambda i, j: (0,) * a.ndim)
    return pl.pallas_call(
        functools.partial(_merge_kernel, ts=ts),
        grid=(b, s // ts),
        in_specs=[pl.BlockSpec((1, ts, d), lambda i, j: (i, j, 0)),
                  pl.BlockSpec((1, HEADS, V_DIM, ts), lambda i, j: (i, 0, 0, j)),
                  pl.BlockSpec((1, ts, CONV_WIDTH), lambda i, j: (i, j, 0)),
                  full(w["g_mla"]), full(w["w_o_attn"]), full(w["w_o_conv"]), full(w["g_ffn"])],
        out_specs=[pl.BlockSpec((1, ts, d), lambda i, j: (i, j, 0)),
                   pl.BlockSpec((1, ts, d), lambda i, j: (i, j, 0))],
        out_shape=[jax.ShapeDtypeStruct((b, s, d), F32), jax.ShapeDtypeStruct((b, s, d), BF16)],
        compiler_params=_params(32, 2),
        name="merge",
    )(x, attn_t, convn, w["g_mla"], w["w_o_attn"], w["w_o_conv"], w["g_ffn"])


def _route_kernel(hn_ref, wq_ref, keys_ref, i_out, j_out, g_out,
                  qt_ref, sv_ref, si_ref, cand_ref, ci_ref, cj_ref, ib_ref, jb_ref, gb_ref, *, tr):
    n_half = 2 * PEER_HEADS
    qt = lax.dot_general(wq_ref[...], hn_ref[...], (((1,), (1,)), ((), ())),
                         preferred_element_type=F32)
    qt_ref[...] = qt.astype(BF16).reshape(n_half, N_KEYS, tr)

    def half_body(hc, _):
        s = jnp.dot(keys_ref[hc], qt_ref[hc], preferred_element_type=F32)
        key_id = lax.broadcasted_iota(I32, (N_KEYS, tr), 0)
        for k in range(PEER_TOPK):
            m = jnp.max(s, axis=0, keepdims=True)
            idx = jnp.min(jnp.where(s == m, key_id, N_KEYS), axis=0, keepdims=True)
            sv_ref[hc, k:k + 1, :] = m
            si_ref[hc, k:k + 1, :] = idx
            s = jnp.where(key_id == idx, -jnp.inf, s)
        return 0

    lax.fori_loop(0, n_half, half_body, 0)

    cand_ref[N_CAND:N_CAND_PAD, :] = jnp.full((N_CAND_PAD - N_CAND, tr), -jnp.inf, F32)
    ci_ref[N_CAND:N_CAND_PAD, :] = jnp.zeros((N_CAND_PAD - N_CAND, tr), I32)
    cj_ref[N_CAND:N_CAND_PAD, :] = jnp.zeros((N_CAND_PAD - N_CAND, tr), I32)

    def head_body(hd, _):
        for c, (a, b) in enumerate(CANDS):
            cand_ref[c:c + 1, :] = sv_ref[2 * hd, a:a + 1, :] + sv_ref[2 * hd + 1, b:b + 1, :]
            ci_ref[c:c + 1, :] = si_ref[2 * hd, a:a + 1, :]
            cj_ref[c:c + 1, :] = si_ref[2 * hd + 1, b:b + 1, :]
        cand = cand_ref[...]
        ci = ci_ref[...]
        cj = cj_ref[...]
        slot = lax.broadcasted_iota(I32, (N_CAND_PAD, tr), 0)
        for k in range(PEER_TOPK):
            m = jnp.max(cand, axis=0, keepdims=True)
            idx = jnp.min(jnp.where(cand == m, slot, N_CAND_PAD), axis=0, keepdims=True)
            sel = slot == idx
            gb_ref[hd, k:k + 1, :] = m
            ib_ref[hd, k:k + 1, :] = jnp.max(jnp.where(sel, ci, -1), axis=0, keepdims=True)
            jb_ref[hd, k:k + 1, :] = jnp.max(jnp.where(sel, cj, -1), axis=0, keepdims=True)
            cand = jnp.where(sel, -jnp.inf, cand)
        best = gb_ref[hd]
        e = jnp.exp(best - best[0:1, :])
        gb_ref[hd] = e / jnp.sum(e, axis=0, keepdims=True)
        return 0

    lax.fori_loop(0, PEER_HEADS, head_body, 0)

    n_pick = PEER_HEADS * PEER_TOPK
    i_out[...] = ib_ref[...].reshape(n_pick, tr).T
    j_out[...] = jb_ref[...].reshape(n_pick, tr).T
    g_out[...] = gb_ref[...].reshape(n_pick, tr).T


def _route(hn, w, *, tr):
    t, d = hn.shape
    n_half = 2 * PEER_HEADS
    n_pick = PEER_HEADS * PEER_TOPK
    full = lambda a: pl.BlockSpec(a.shape, lambda i: (0,) * a.ndim)
    pick = pl.BlockSpec((tr, n_pick), lambda i: (i, 0))
    return pl.pallas_call(
        functools.partial(_route_kernel, tr=tr),
        grid=(t // tr,),
        in_specs=[pl.BlockSpec((tr, d), lambda i: (i, 0)), full(w["w_pq_t"]), full(w["keys"])],
        out_specs=[pick, pick, pick],
        out_shape=[jax.ShapeDtypeStruct((t, n_pick), I32), jax.ShapeDtypeStruct((t, n_pick), I32),
                   jax.ShapeDtypeStruct((t, n_pick), F32)],
        scratch_shapes=[pltpu.VMEM((n_half, N_KEYS, tr), BF16),
                        pltpu.VMEM((n_half, PEER_TOPK, tr), F32),
                        pltpu.VMEM((n_half, PEER_TOPK, tr), I32),
                        pltpu.VMEM((N_CAND_PAD, tr), F32),
                        pltpu.VMEM((N_CAND_PAD, tr), I32),
                        pltpu.VMEM((N_CAND_PAD, tr), I32),
                        pltpu.VMEM((PEER_HEADS, PEER_TOPK, tr), I32),
                        pltpu.VMEM((PEER_HEADS, PEER_TOPK, tr), I32),
                        pltpu.VMEM((PEER_HEADS, PEER_TOPK, tr), F32)],
        compiler_params=_params(32, 1),
        name="route",
    )(hn, w["w_pq_t"], w["keys"])


def _expert_kernel(hn_ref, x1_ref, i_ref, j_ref, g_ref, u_ref, v_ref, gfin_ref, y_ref,
                   w_ref, acc_ref, *, tt, ni, n_et):
    ne = pl.program_id(1)

    @pl.when(ne == 0)
    def _():
        acc_ref[...] = jnp.zeros_like(acc_ref)
        key_id = lax.broadcasted_iota(I32, (N_KEYS, N_KEYS), 0)

        def tok(t, _):
            irow = i_ref[pl.ds(t, 1), :]
            jrow = j_ref[pl.ds(t, 1), :]
            grow = g_ref[pl.ds(t, 1), :]
            at = jnp.where(key_id == irow, grow, 0.0).astype(BF16)
            bt = jnp.where(key_id == jrow, 1.0, 0.0).astype(BF16)
            w = lax.dot_general(at, bt, (((1,), (1,)), ((), ())), preferred_element_type=F32)
            w_ref[pl.ds(pl.multiple_of(t * W_PITCH, SUBLANES), N_KEYS), :] = w
            return 0

        lax.fori_loop(0, tt, tok, 0)

    a = lax.dot_general(hn_ref[...], u_ref[...], (((1,), (1,)), ((), ())),
                        preferred_element_type=F32)
    act = 0.5 * a * (1.0 + lax.erf(a * np.float32(np.sqrt(0.5))))
    w = jnp.concatenate(
        [w_ref[pl.ds(ne * ni + ii, tt, stride=W_PITCH), :] for ii in range(ni)], axis=1)
    acc_ref[...] += jnp.dot((w * act).astype(BF16), v_ref[...], preferred_element_type=F32)

    @pl.when(ne == n_et - 1)
    def _():
        y_ref[...] = _rms(x1_ref[...] + acc_ref[...], gfin_ref[...])


def _experts(hn, x1, pick_i, pick_j, pick_g, w, g_final, *, tt, ni):
    t, d = hn.shape
    n_exp = w["u"].shape[0]
    et = ni * N_KEYS
    n_et = n_exp // et
    n_pick = PEER_HEADS * PEER_TOPK
    tok = lambda width: pl.BlockSpec((tt, width), lambda i, e: (i, 0))
    return pl.pallas_call(
        functools.partial(_expert_kernel, tt=tt, ni=ni, n_et=n_et),
        grid=(t // tt, n_et),
        in_specs=[tok(d), tok(d), tok(n_pick), tok(n_pick), tok(n_pick),
                  pl.BlockSpec((et, d), lambda i, e: (e, 0)),
                  pl.BlockSpec((et, d), lambda i, e: (e, 0)),
                  pl.BlockSpec((1, d), lambda i, e: (0, 0))],
        out_specs=tok(d),
        out_shape=jax.ShapeDtypeStruct((t, d), F32),
        scratch_shapes=[pltpu.VMEM((tt * W_PITCH, N_KEYS), F32), pltpu.VMEM((tt, d), F32)],
        compiler_params=_params(56, 2),
        name="experts",
    )(hn, x1, pick_i, pick_j, pick_g, w["u"], w["v"], g_final)


def _rope_tables(pos, scale):
    freqs = ROPE_THETA ** (-jnp.arange(ROPE_HALF, dtype=F32) / ROPE_HALF)
    ang = pos.astype(F32)[:, None] * freqs[None, :]
    cos, sin = jnp.cos(ang), jnp.sin(ang)
    n = pos.shape[0]
    ones = jnp.ones((n, QK_NOPE), F32)
    tail = jnp.ones((n, HEAD_PAD - QK_NOPE - QK_ROPE), F32)
    cos_tab = jnp.concatenate([ones, cos, cos, tail], axis=1)
    sin_tab = jnp.concatenate([0 * ones, -sin, sin, 0 * tail], axis=1)
    return cos_tab * scale, sin_tab * scale, cos_tab, sin_tab


def _prep_weights(l, g_attn_norm, w_in, g_q, w_uq, g_kv, w_ukv, w_conv, g_mla_out, g_conv_out,
                  w_out, g_ffn_norm, w_peer_q, peer_keys, peer_u, peer_v):
    d = w_in.shape[1]
    zeros = lambda n: jnp.zeros((d, n), w_in.dtype)
    kpe_end = KV_RANK + Q_RANK + QK_ROPE
    w_in_r = jnp.concatenate(
        [w_in[l][:, :CKV0 + KV_RANK], zeros(ROPE_LO), w_in[l][:, CKV0 + KV_RANK:kpe_end],
         zeros(HEAD_PAD - ROPE_LO - QK_ROPE), w_in[l][:, kpe_end:]], axis=1)
    w_uq_r = jnp.pad(w_uq[l].reshape(Q_RANK, HEADS, QK_NOPE + QK_ROPE),
                     ((0, 0), (0, 0), (0, HEAD_PAD - QK_NOPE - QK_ROPE)))
    kv3 = w_ukv[l].reshape(KV_RANK, HEADS, QK_NOPE + V_DIM)
    w_k = jnp.pad(kv3[:, :, :QK_NOPE], ((0, 0), (0, 0), (0, HEAD_PAD - QK_NOPE)))
    w_vt = kv3[:, :, QK_NOPE:].reshape(KV_RANK, HEADS * V_DIM).T
    place = jnp.zeros((QK_ROPE, HEAD_PAD), F32).at[
        jnp.arange(QK_ROPE), ROPE_LO + jnp.arange(QK_ROPE)].set(1.0)
    row = lambda g: g[l].reshape(1, -1).astype(F32)
    mla_w = HEADS * V_DIM
    return {
        "w_in": w_in_r.astype(BF16), "g_attn": row(g_attn_norm), "g_q": row(g_q),
        "w_uq": w_uq_r.reshape(Q_RANK, HEADS * HEAD_PAD).astype(BF16), "g_kv": row(g_kv),
        "w_conv": w_conv[l].astype(F32), "g_conv": row(g_conv_out),
        "w_k": w_k.reshape(KV_RANK, HEADS * HEAD_PAD).astype(BF16), "w_vt": w_vt.astype(BF16),
        "place": place.astype(BF16),
        "g_mla": row(g_mla_out), "w_o_attn": w_out[l][:mla_w].astype(BF16),
        "w_o_conv": w_out[l][mla_w:].astype(BF16), "g_ffn": row(g_ffn_norm),
        "w_pq_t": w_peer_q[l].T.astype(BF16),
        "keys": peer_keys[l].reshape(2 * PEER_HEADS, N_KEYS, -1).astype(BF16),
        "u": peer_u[l].astype(BF16), "v": peer_v[l].astype(BF16),
    }


def _tile(n, pref):
    t = min(n, pref)
    assert n % t == 0, (n, pref)
    return t


def _layer(x, pos0, s_valid, past_kv, past_kpe, past_conv, w, g_final):
    b, s, d = x.shape
    scale = float(QK_NOPE + QK_ROPE) ** -0.5
    tabs = _rope_tables(pos0 + jnp.arange(s, dtype=jnp.int32), scale)
    if past_conv is None:
        past_conv = jnp.zeros((b, 2, CONV_WIDTH), F32)
    q, ckv, kpe, convn, new_conv = _front(x, past_conv, tabs, w, ts=_tile(s, 512), s_valid=s_valid)

    tk = 256
    if past_kv is None:
        ckv_all, kpe_all, sk_valid = ckv, kpe, s
    else:
        ckv_all = jnp.concatenate([past_kv, ckv[:, :s_valid]], axis=1)
        kpe_all = jnp.concatenate([past_kpe, kpe[:, :s_valid]], axis=1)
        sk_valid = ckv_all.shape[1]
    pad = -sk_valid % tk
    if pad:
        ckv_all = jnp.pad(ckv_all, ((0, 0), (0, pad), (0, 0)))
        kpe_all = jnp.pad(kpe_all, ((0, 0), (0, pad), (0, 0)))
    k, vt = _kv_up(ckv_all, kpe_all, w, tk=tk)
    attn_t = _attention(q, k, vt, tq=_tile(s, 256), q_off=pos0, sk_valid=sk_valid)
    x1, hn = _merge(x, attn_t, convn, w, ts=_tile(s, 512))

    t = b * s
    hn2 = hn.reshape(t, d)
    pick_i, pick_j, pick_g = _route(hn2, w, tr=_tile(t, 256))
    y = _experts(hn2, x1.reshape(t, d), pick_i, pick_j, pick_g, w, g_final, tt=_tile(t, 256), ni=8)
    return y.reshape(b, s, d), ckv, kpe, new_conv


def kernel(x_prompt, x_sample, cache_kv_latent, cache_k_rope, state_conv, g_attn_norm, w_in, g_q, w_uq, g_kv, w_ukv, w_conv, g_mla_out, g_conv_out, w_out, g_ffn_norm, w_peer_q, peer_keys, peer_u, peer_v, g_final):
    depth = w_in.shape[0]
    assert depth == 1, "the final norm is fused into the last layer; one layer supported"
    past_len = cache_kv_latent.shape[2]
    s_dec = x_sample.shape[1]
    s_pad = -(-s_dec // LANES) * LANES
    hs = jnp.pad(x_sample, ((0, 0), (0, s_pad - s_dec), (0, 0)))
    gfin = g_final.reshape(1, -1).astype(F32)
    w = _prep_weights(0, g_attn_norm, w_in, g_q, w_uq, g_kv, w_ukv, w_conv, g_mla_out, g_conv_out,
                      w_out, g_ffn_norm, w_peer_q, peer_keys, peer_u, peer_v)
    yp, kv_p, kpe_p, conv_p = _layer(x_prompt, 0, x_prompt.shape[1], None, None, None, w, gfin)
    ys, kv_s, kpe_s, conv_s = _layer(hs, past_len, s_dec, cache_kv_latent[0], cache_k_rope[0],
                                     state_conv[0], w, gfin)
    return (yp, ys[:, :s_dec],
            kv_p[None], kpe_p[None], conv_p[None],
            kv_s[None, :, :s_dec], kpe_s[None, :, :s_dec], conv_s[None])
```

```python
import functools

import numpy as np
import jax
import jax.numpy as jnp
from jax import lax
from jax.experimental import pallas as pl
from jax.experimental.pallas import tpu as pltpu

F32 = jnp.float32
BF16 = jnp.bfloat16
I32 = jnp.int32

EPS = 1e-6
NEG_INF = -1e30
CHUNK_SHIFT = 6
HEADS = 8
QK_NOPE = 64
QK_ROPE = 32
V_DIM = 64
VT_ROWS = 80
Q_RANK = 384
KV_RANK = 256
CONV_WIDTH = 512
ROPE_THETA = 10000.0
HEAD_PAD = 128
ROPE_LO = QK_NOPE
ROPE_HALF = QK_ROPE // 2
PEER_HEADS = 8
N_KEYS = 128
PEER_TOPK = 16
LANES = 128
SUBLANES = 8
W_PITCH = 136

CQ0, CKV0, KPE0, UC0, BG0, CG0, ZW = 0, 384, 640, 768, 1280, 1792, 2304

CANDS = sorted(
    [(a, b) for a in range(PEER_TOPK) for b in range(PEER_TOPK) if (a + 1) * (b + 1) <= PEER_TOPK],
    key=lambda ab: ab[0] * PEER_TOPK + ab[1])
N_CAND = len(CANDS)
N_CAND_PAD = -(-N_CAND // SUBLANES) * SUBLANES


def _rms(x, g):
    ms = jnp.mean(x * x, axis=-1, keepdims=True)
    return x * lax.rsqrt(ms + EPS) * g


def _swap_rope_halves(v):
    lane = lax.broadcasted_iota(I32, v.shape, 1)
    return jnp.where(lane < ROPE_LO + ROPE_HALF,
                     pltpu.roll(v, LANES - ROPE_HALF, 1), pltpu.roll(v, ROPE_HALF, 1))


def _params(vmem_mb, n_axes):
    return pltpu.CompilerParams(dimension_semantics=("arbitrary",) * n_axes,
                                vmem_limit_bytes=vmem_mb * 1024 * 1024)


def _front_kernel(x_ref, win_ref, gattn_ref, gq_ref, wuq_ref, gkv_ref, wconv_ref, gconv_ref,
                  cq_ref, sq_ref, ck_ref, sk_ref, past_ref,
                  q_ref, ckv_ref, kpe_ref, convn_ref, newconv_ref, carry_ref,
                  *, ts, nc_tile, nc_row):
    si = pl.program_id(1)
    x = x_ref[0]
    h = _rms(x, gattn_ref[...]).astype(BF16)
    z = jnp.dot(h, win_ref[...], preferred_element_type=F32)

    cqn = _rms(z[:, CQ0:CKV0], gq_ref[...]).astype(BF16)
    q = jnp.dot(cqn, wuq_ref[...], preferred_element_type=F32)
    cq = cq_ref[...]
    sq = sq_ref[...]
    for hd in range(HEADS):
        qh = q[:, hd * HEAD_PAD:(hd + 1) * HEAD_PAD]
        q_ref[0, hd] = (qh * cq + _swap_rope_halves(qh) * sq).astype(BF16)

    ckv_ref[0] = _rms(z[:, CKV0:KPE0], gkv_ref[...])

    kp = z[:, KPE0:UC0]
    kp = kp * ck_ref[...] + _swap_rope_halves(kp) * sk_ref[...]
    kpe_ref[0] = kp[:, ROPE_LO:ROPE_LO + QK_ROPE]

    u = z[:, CG0:ZW] * z[:, UC0:BG0]

    @pl.when(si == 0)
    def _():
        carry_ref[0:2, :] = past_ref[0]

    prev2 = carry_ref[0:1, :]
    prev1 = carry_ref[1:2, :]
    row = lax.broadcasted_iota(I32, u.shape, 0)
    u1 = jnp.where(row == 0, prev1, pltpu.roll(u, 1, 0))
    u2 = jnp.where(row == 0, prev2, jnp.where(row == 1, prev1, pltpu.roll(u, 2, 0)))
    conv = u2 * wconv_ref[0:1, :] + u1 * wconv_ref[1:2, :] + u * wconv_ref[2:3, :]
    convn_ref[0] = _rms(z[:, BG0:CG0] * conv, gconv_ref[...]).astype(BF16)
    carry_ref[0:2, :] = u[ts - 2:ts, :]

    @pl.when(si == nc_tile)
    def _():
        newconv_ref[0] = u[nc_row:nc_row + 2, :]


def _front(x, past_conv, tabs, w, *, ts, s_valid):
    b, s, d = x.shape
    cq, sq, ck, sk = tabs
    full = lambda a: pl.BlockSpec(a.shape, lambda i, j: (0,) * a.ndim)
    tab = pl.BlockSpec((ts, LANES), lambda i, j: (j, 0))
    kern = functools.partial(_front_kernel, ts=ts, nc_tile=(s_valid - 2) // ts,
                             nc_row=(s_valid - 2) % ts)
    return pl.pallas_call(
        kern,
        grid=(b, s // ts),
        in_specs=[pl.BlockSpec((1, ts, d), lambda i, j: (i, j, 0)),
                  full(w["w_in"]), full(w["g_attn"]), full(w["g_q"]), full(w["w_uq"]),
                  full(w["g_kv"]), full(w["w_conv"]), full(w["g_conv"]),
                  tab, tab, tab, tab,
                  pl.BlockSpec((1, 2, CONV_WIDTH), lambda i, j: (i, 0, 0))],
        out_specs=[pl.BlockSpec((1, HEADS, ts, HEAD_PAD), lambda i, j: (i, 0, j, 0)),
                   pl.BlockSpec((1, ts, KV_RANK), lambda i, j: (i, j, 0)),
                   pl.BlockSpec((1, ts, QK_ROPE), lambda i, j: (i, j, 0)),
                   pl.BlockSpec((1, ts, CONV_WIDTH), lambda i, j: (i, j, 0)),
                   pl.BlockSpec((1, 2, CONV_WIDTH), lambda i, j: (i, 0, 0))],
        out_shape=[jax.ShapeDtypeStruct((b, HEADS, s, HEAD_PAD), BF16),
                   jax.ShapeDtypeStruct((b, s, KV_RANK), F32),
                   jax.ShapeDtypeStruct((b, s, QK_ROPE), F32),
                   jax.ShapeDtypeStruct((b, s, CONV_WIDTH), BF16),
                   jax.ShapeDtypeStruct((b, 2, CONV_WIDTH), F32)],
        scratch_shapes=[pltpu.VMEM((SUBLANES, CONV_WIDTH), F32)],
        compiler_params=_params(48, 2),
        name="front",
    )(x, w["w_in"], w["g_attn"], w["g_q"], w["w_uq"], w["g_kv"], w["w_conv"], w["g_conv"],
      cq, sq, ck, sk, past_conv)


def _kvup_kernel(ckv_ref, kpe_ref, wk_ref, wvt_ref, place_ref, k_ref, vt_ref):
    c = ckv_ref[0].astype(BF16)
    kn = jnp.dot(c, wk_ref[...], preferred_element_type=F32)
    kp = jnp.dot(kpe_ref[0].astype(BF16), place_ref[...], preferred_element_type=F32)
    for hd in range(HEADS):
        k_ref[0, hd] = (kn[:, hd * HEAD_PAD:(hd + 1) * HEAD_PAD] + kp).astype(BF16)
    vt = lax.dot_general(wvt_ref[...], c, (((1,), (1,)), ((), ())),
                         preferred_element_type=F32)
    ones_row = jnp.where(lax.broadcasted_iota(I32, (VT_ROWS - V_DIM, vt.shape[1]), 0) == 0,
                         1.0, 0.0).astype(BF16)
    for hd in range(HEADS):
        vt_ref[0, hd, 0, 0:V_DIM, :] = vt[hd * V_DIM:(hd + 1) * V_DIM, :].astype(BF16)
        vt_ref[0, hd, 0, V_DIM:VT_ROWS, :] = ones_row


def _kv_up(ckv, kpe, w, *, tk):
    b, sk, _ = ckv.shape
    nkb = sk // tk
    full = lambda a: pl.BlockSpec(a.shape, lambda i, j: (0,) * a.ndim)
    return pl.pallas_call(
        _kvup_kernel,
        grid=(b, nkb),
        in_specs=[pl.BlockSpec((1, tk, KV_RANK), lambda i, j: (i, j, 0)),
                  pl.BlockSpec((1, tk, QK_ROPE), lambda i, j: (i, j, 0)),
                  full(w["w_k"]), full(w["w_vt"]), full(w["place"])],
        out_specs=[pl.BlockSpec((1, HEADS, tk, HEAD_PAD), lambda i, j: (i, 0, j, 0)),
                   pl.BlockSpec((1, HEADS, 1, VT_ROWS, tk), lambda i, j: (i, 0, j, 0, 0))],
        out_shape=[jax.ShapeDtypeStruct((b, HEADS, sk, HEAD_PAD), BF16),
                   jax.ShapeDtypeStruct((b, HEADS, nkb, VT_ROWS, tk), BF16)],
        compiler_params=_params(32, 2),
        name="kv_up",
    )(ckv, kpe, w["w_k"], w["w_vt"], w["place"])


def _attn_kernel(q_ref, k_ref, vt_ref, o_ref, sa_ref, sb_ref, *, tq, tk, nkb, q_off, sk_valid):
    qi = pl.program_id(2)
    q = q_ref[0, 0]
    q0 = q_off + qi * tq
    qch = lax.shift_right_logical(q0 + lax.broadcasted_iota(I32, (1, tq), 1), CHUNK_SHIFT)
    last_visible = lax.shift_left(lax.shift_right_logical(q0 + tq - 1, CHUNK_SHIFT) + 1,
                                  CHUNK_SHIFT) - 1
    nb = jnp.minimum(nkb, last_visible // tk + 1)
    first_chunk_end = lax.shift_left(lax.shift_right_logical(q0, CHUNK_SHIFT) + 1, CHUNK_SHIFT)
    n_full = jnp.minimum(jnp.minimum(first_chunk_end // tk, sk_valid // tk), nb)

    def scores(kb):
        k = k_ref[0, 0, pl.ds(pl.multiple_of(kb * tk, tk), tk), :]
        return lax.dot_general(k, q, (((1,), (1,)), ((), ())), preferred_element_type=F32)

    def update(s_ref, kb, carry):
        m, acc = carry
        m_new = jnp.maximum(m, jnp.max(s_ref[...], axis=0, keepdims=True))
        p = jnp.exp2(s_ref[...] - m_new).astype(BF16)
        acc = jnp.exp2(m - m_new) * acc + jnp.dot(vt_ref[0, 0, kb], p,
                                                  preferred_element_type=F32)
        return m_new, acc

    def pair(j, carry):
        sb_ref[...] = scores(2 * j + 1)
        carry = update(sa_ref, 2 * j, carry)
        sa_ref[...] = scores(jnp.minimum(2 * j + 2, nkb - 1))
        return update(sb_ref, 2 * j + 1, carry)

    def masked_update(s_ref, kb, carry):
        kpos = kb * tk + lax.broadcasted_iota(I32, (tk, 1), 0)
        vis = (lax.shift_right_logical(kpos, CHUNK_SHIFT) <= qch) & (kpos < sk_valid)
        s_ref[...] = jnp.where(vis, s_ref[...], NEG_INF)
        return update(s_ref, kb, carry)

    n_pairs = n_full // 2
    base = 2 * n_pairs
    rest = nb - base

    def tail_one(_, carry):
        return masked_update(sa_ref, base, carry)

    def tail_two(_, carry):
        sb_ref[...] = scores(base + 1)
        carry = masked_update(sa_ref, base, carry)
        return masked_update(sb_ref, base + 1, carry)

    def tail_more(kb, carry):
        sa_ref[...] = scores(kb)
        return masked_update(sa_ref, kb, carry)

    sa_ref[...] = scores(0)
    carry = (jnp.full((1, tq), NEG_INF, F32), jnp.zeros((VT_ROWS, tq), F32))
    carry = lax.fori_loop(0, n_pairs, pair, carry)
    carry = lax.fori_loop(0, (rest == 1).astype(I32), tail_one, carry)
    carry = lax.fori_loop(0, (rest >= 2).astype(I32), tail_two, carry)
    _, acc = lax.fori_loop(base + 2, nb, tail_more, carry)
    o_ref[0, 0] = acc[:V_DIM] / acc[V_DIM:V_DIM + 1]


def _attention(q, k, vt, *, tq, q_off, sk_valid):
    b, _, s, _ = q.shape
    _, _, nkb, _, tk = vt.shape
    sk = k.shape[2]
    kern = functools.partial(_attn_kernel, tq=tq, tk=tk, nkb=nkb, q_off=q_off, sk_valid=sk_valid)
    return pl.pallas_call(
        kern,
        grid=(b, HEADS, s // tq),
        in_specs=[pl.BlockSpec((1, 1, tq, HEAD_PAD), lambda i, h, j: (i, h, j, 0)),
                  pl.BlockSpec((1, 1, sk, HEAD_PAD), lambda i, h, j: (i, h, 0, 0)),
                  pl.BlockSpec((1, 1, nkb, VT_ROWS, tk), lambda i, h, j: (i, h, 0, 0, 0))],
        out_specs=pl.BlockSpec((1, 1, V_DIM, tq), lambda i, h, j: (i, h, 0, j)),
        out_shape=jax.ShapeDtypeStruct((b, HEADS, V_DIM, s), F32),
        scratch_shapes=[pltpu.VMEM((tk, tq), F32), pltpu.VMEM((tk, tq), F32)],
        compiler_params=_params(32, 3),
        name="attn",
    )(q, k, vt)


def _merge_kernel(x_ref, at_ref, cn_ref, gm_ref, woa_ref, woc_ref, gf_ref, x1_ref, hn_ref, *, ts):
    a = at_ref[0].reshape(HEADS * V_DIM, ts).T
    an = _rms(a, gm_ref[...]).astype(BF16)
    mix = (jnp.dot(an, woa_ref[...], preferred_element_type=F32)
           + jnp.dot(cn_ref[0], woc_ref[...], preferred_element_type=F32))
    x1 = x_ref[0] + mix
    x1_ref[0] = x1
    hn_ref[0] = _rms(x1, gf_ref[...]).astype(BF16)


def _merge(x, attn_t, convn, w, *, ts):
    b, s, d = x.shape
    full = lambda a: pl.BlockSpec(a.shape, lambda i, j: (0,) * a.ndim)
    return pl.pallas_call(
        functools.partial(_merge_kernel, ts=ts),
        grid=(b, s // ts),
        in_specs=[pl.BlockSpec((1, ts, d), lambda i, j: (i, j, 0)),
                  pl.BlockSpec((1, HEADS, V_DIM, ts), lambda i, j: (i, 0, 0, j)),
                  pl.BlockSpec((1, ts, CONV_WIDTH), lambda i, j: (i, j, 0)),
                  full(w["g_mla"]), full(w["w_o_attn"]), full(w["w_o_conv"]), full(w["g_ffn"])],
        out_specs=[pl.BlockSpec((1, ts, d), lambda i, j: (i, j, 0)),
                   pl.BlockSpec((1, ts, d), lambda i, j: (i, j, 0))],
        out_shape=[jax.ShapeDtypeStruct((b, s, d), F32), jax.ShapeDtypeStruct((b, s, d), BF16)],
        compiler_params=_params(32, 2),
        name="merge",
    )(x, attn_t, convn, w["g_mla"], w["w_o_attn"], w["w_o_conv"], w["g_ffn"])


def _route_kernel(hn_ref, wq_ref, keys_ref, i_out, j_out, g_out,
                  qt_ref, sv_ref, si_ref, cand_ref, ce_ref, eb_ref, gb_ref, *, tr):
    n_half = 2 * PEER_HEADS
    qt = lax.dot_general(wq_ref[...], hn_ref[...], (((1,), (1,)), ((), ())),
                         preferred_element_type=F32)
    qt_ref[...] = qt.astype(BF16).reshape(n_half, N_KEYS, tr)

    def half_body(hc, _):
        s = jnp.dot(keys_ref[hc], qt_ref[hc], preferred_element_type=F32)
        key_id = lax.broadcasted_iota(I32, (N_KEYS, tr), 0)
        for k in range(PEER_TOPK):
            m = jnp.max(s, axis=0, keepdims=True)
            idx = jnp.min(jnp.where(s == m, key_id, N_KEYS), axis=0, keepdims=True)
            sv_ref[hc, k:k + 1, :] = m
            si_ref[hc, k:k + 1, :] = idx
            s = jnp.where(key_id == idx, -jnp.inf, s)
        return 0

    lax.fori_loop(0, n_half, half_body, 0)

    cand_ref[N_CAND:N_CAND_PAD, :] = jnp.full((N_CAND_PAD - N_CAND, tr), -jnp.inf, F32)
    ce_ref[N_CAND:N_CAND_PAD, :] = jnp.zeros((N_CAND_PAD - N_CAND, tr), I32)

    def head_body(hd, _):
        for c, (a, b) in enumerate(CANDS):
            cand_ref[c:c + 1, :] = sv_ref[2 * hd, a:a + 1, :] + sv_ref[2 * hd + 1, b:b + 1, :]
            ce_ref[c:c + 1, :] = (si_ref[2 * hd, a:a + 1, :] * N_KEYS
                                  + si_ref[2 * hd + 1, b:b + 1, :])
        cand = cand_ref[...]
        ce = ce_ref[...]
        slot = lax.broadcasted_iota(I32, (N_CAND_PAD, tr), 0)
        for k in range(PEER_TOPK):
            m = jnp.max(cand, axis=0, keepdims=True)
            idx = jnp.min(jnp.where(cand == m, slot, N_CAND_PAD), axis=0, keepdims=True)
            sel = slot == idx
            gb_ref[hd, k:k + 1, :] = m
            eb_ref[hd, k:k + 1, :] = jnp.max(jnp.where(sel, ce, -1), axis=0, keepdims=True)
            cand = jnp.where(sel, -jnp.inf, cand)
        best = gb_ref[hd]
        e = jnp.exp(best - best[0:1, :])
        gb_ref[hd] = e / jnp.sum(e, axis=0, keepdims=True)
        return 0

    lax.fori_loop(0, PEER_HEADS, head_body, 0)

    n_pick = PEER_HEADS * PEER_TOPK
    experts = eb_ref[...].reshape(n_pick, tr).T
    i_out[...] = lax.shift_right_logical(experts, 7)
    j_out[...] = experts & (N_KEYS - 1)
    g_out[...] = gb_ref[...].reshape(n_pick, tr).T


def _route(hn, w, *, tr):
    t, d = hn.shape
    n_half = 2 * PEER_HEADS
    n_pick = PEER_HEADS * PEER_TOPK
    full = lambda a: pl.BlockSpec(a.shape, lambda i: (0,) * a.ndim)
    pick = pl.BlockSpec((tr, n_pick), lambda i: (i, 0))
    return pl.pallas_call(
        functools.partial(_route_kernel, tr=tr),
        grid=(t // tr,),
        in_specs=[pl.BlockSpec((tr, d), lambda i: (i, 0)), full(w["w_pq_t"]), full(w["keys"])],
        out_specs=[pick, pick, pick],
        out_shape=[jax.ShapeDtypeStruct((t, n_pick), I32), jax.ShapeDtypeStruct((t, n_pick), I32),
                   jax.ShapeDtypeStruct((t, n_pick), F32)],
        scratch_shapes=[pltpu.VMEM((n_half, N_KEYS, tr), BF16),
                        pltpu.VMEM((n_half, PEER_TOPK, tr), F32),
                        pltpu.VMEM((n_half, PEER_TOPK, tr), I32),
                        pltpu.VMEM((N_CAND_PAD, tr), F32),
                        pltpu.VMEM((N_CAND_PAD, tr), I32),
                        pltpu.VMEM((PEER_HEADS, PEER_TOPK, tr), I32),
                        pltpu.VMEM((PEER_HEADS, PEER_TOPK, tr), F32)],
        compiler_params=_params(32, 1),
        name="route",
    )(hn, w["w_pq_t"], w["keys"])


def _expert_kernel(hn_ref, x1_ref, i_ref, j_ref, g_ref, u_ref, v_ref, gfin_ref, y_ref,
                   w_ref, acc_ref, *, tt, ni, n_et):
    ne = pl.program_id(1)

    @pl.when(ne == 0)
    def _():
        acc_ref[...] = jnp.zeros_like(acc_ref)
        key_id = lax.broadcasted_iota(I32, (N_KEYS, N_KEYS), 0)
        zero = jnp.zeros((N_KEYS, N_KEYS), BF16)

        def one_hots(t):
            irow = i_ref[pl.ds(t, 1), :]
            jrow = j_ref[pl.ds(t, 1), :]
            grow = g_ref[pl.ds(t, 1), :]
            at = jnp.where(key_id == irow, grow, 0.0).astype(BF16)
            bt = jnp.where(key_id == jrow, 1.0, 0.0).astype(BF16)
            return at, bt

        def token_pair(tp, _):
            t0 = 2 * tp
            at0, bt0 = one_hots(t0)
            at1, bt1 = one_hots(t0 + 1)
            lhs = jnp.concatenate([at0, at1], axis=1)
            rhs = jnp.concatenate([jnp.concatenate([bt0, zero], axis=1),
                                   jnp.concatenate([zero, bt1], axis=1)], axis=0)
            w = lax.dot_general(lhs, rhs, (((1,), (1,)), ((), ())), preferred_element_type=F32)
            row0 = pl.multiple_of(t0 * W_PITCH, SUBLANES)
            w_ref[pl.ds(row0, N_KEYS), :] = w[:, :N_KEYS]
            w_ref[pl.ds(row0 + W_PITCH, N_KEYS), :] = w[:, N_KEYS:]
            return 0

        lax.fori_loop(0, tt // 2, token_pair, 0, unroll=16)

    a = lax.dot_general(hn_ref[...], u_ref[...], (((1,), (1,)), ((), ())),
                        preferred_element_type=F32)
    act = 0.5 * a * (1.0 + lax.erf(a * np.float32(np.sqrt(0.5))))
    w = jnp.concatenate(
        [w_ref[pl.ds(ne * ni + ii, tt, stride=W_PITCH), :] for ii in range(ni)], axis=1)
    acc_ref[...] += jnp.dot((w * act).astype(BF16), v_ref[...], preferred_element_type=F32)

    @pl.when(ne == n_et - 1)
    def _():
        y_ref[...] = _rms(x1_ref[...] + acc_ref[...], gfin_ref[...])


def _experts(hn, x1, pick_i, pick_j, pick_g, w, g_final, *, tt, ni):
    t, d = hn.shape
    n_exp = w["u"].shape[0]
    et = ni * N_KEYS
    n_et = n_exp // et
    n_pick = PEER_HEADS * PEER_TOPK
    tok = lambda width: pl.BlockSpec((tt, width), lambda i, e: (i, 0))
    return pl.pallas_call(
        functools.partial(_expert_kernel, tt=tt, ni=ni, n_et=n_et),
        grid=(t // tt, n_et),
        in_specs=[tok(d), tok(d), tok(n_pick), tok(n_pick), tok(n_pick),
                  pl.BlockSpec((et, d), lambda i, e: (e, 0)),
                  pl.BlockSpec((et, d), lambda i, e: (e, 0)),
                  pl.BlockSpec((1, d), lambda i, e: (0, 0))],
        out_specs=tok(d),
        out_shape=jax.ShapeDtypeStruct((t, d), F32),
        scratch_shapes=[pltpu.VMEM((tt * W_PITCH, N_KEYS), F32), pltpu.VMEM((tt, d), F32)],
        compiler_params=_params(56, 2),
        name="experts",
    )(hn, x1, pick_i, pick_j, pick_g, w["u"], w["v"], g_final)


def _rope_tables(pos, scale):
    freqs = ROPE_THETA ** (-jnp.arange(ROPE_HALF, dtype=F32) / ROPE_HALF)
    ang = pos.astype(F32)[:, None] * freqs[None, :]
    cos, sin = jnp.cos(ang), jnp.sin(ang)
    n = pos.shape[0]
    ones = jnp.ones((n, QK_NOPE), F32)
    tail = jnp.ones((n, HEAD_PAD - QK_NOPE - QK_ROPE), F32)
    cos_tab = jnp.concatenate([ones, cos, cos, tail], axis=1)
    sin_tab = jnp.concatenate([0 * ones, -sin, sin, 0 * tail], axis=1)
    return cos_tab * scale, sin_tab * scale, cos_tab, sin_tab


def _prep_weights(l, g_attn_norm, w_in, g_q, w_uq, g_kv, w_ukv, w_conv, g_mla_out, g_conv_out,
                  w_out, g_ffn_norm, w_peer_q, peer_keys, peer_u, peer_v):
    d = w_in.shape[1]
    zeros = lambda n: jnp.zeros((d, n), w_in.dtype)
    kpe_end = KV_RANK + Q_RANK + QK_ROPE
    w_in_r = jnp.concatenate(
        [w_in[l][:, :CKV0 + KV_RANK], zeros(ROPE_LO), w_in[l][:, CKV0 + KV_RANK:kpe_end],
         zeros(HEAD_PAD - ROPE_LO - QK_ROPE), w_in[l][:, kpe_end:]], axis=1)
    w_uq_r = jnp.pad(w_uq[l].reshape(Q_RANK, HEADS, QK_NOPE + QK_ROPE),
                     ((0, 0), (0, 0), (0, HEAD_PAD - QK_NOPE - QK_ROPE)))
    kv3 = w_ukv[l].reshape(KV_RANK, HEADS, QK_NOPE + V_DIM)
    w_k = jnp.pad(kv3[:, :, :QK_NOPE], ((0, 0), (0, 0), (0, HEAD_PAD - QK_NOPE)))
    w_vt = kv3[:, :, QK_NOPE:].reshape(KV_RANK, HEADS * V_DIM).T
    place = jnp.zeros((QK_ROPE, HEAD_PAD), F32).at[
        jnp.arange(QK_ROPE), ROPE_LO + jnp.arange(QK_ROPE)].set(1.0)
    row = lambda g: g[l].reshape(1, -1).astype(F32)
    mla_w = HEADS * V_DIM
    return {
        "w_in": w_in_r.astype(BF16), "g_attn": row(g_attn_norm), "g_q": row(g_q),
        "w_uq": w_uq_r.reshape(Q_RANK, HEADS * HEAD_PAD).astype(BF16), "g_kv": row(g_kv),
        "w_conv": w_conv[l].astype(F32), "g_conv": row(g_conv_out),
        "w_k": w_k.reshape(KV_RANK, HEADS * HEAD_PAD).astype(BF16), "w_vt": w_vt.astype(BF16),
        "place": place.astype(BF16),
        "g_mla": row(g_mla_out), "w_o_attn": w_out[l][:mla_w].astype(BF16),
        "w_o_conv": w_out[l][mla_w:].astype(BF16), "g_ffn": row(g_ffn_norm),
        "w_pq_t": w_peer_q[l].T.astype(BF16),
        "keys": peer_keys[l].reshape(2 * PEER_HEADS, N_KEYS, -1).astype(BF16),
        "u": peer_u[l].astype(BF16), "v": peer_v[l].astype(BF16),
    }


def _tile(n, pref):
    t = min(n, pref)
    assert n % t == 0, (n, pref)
    return t


def _layer(x, pos0, s_valid, past_kv, past_kpe, past_conv, w, g_final):
    b, s, d = x.shape
    scale = float(QK_NOPE + QK_ROPE) ** -0.5 * float(np.log2(np.e))
    tabs = _rope_tables(pos0 + jnp.arange(s, dtype=jnp.int32), scale)
    if past_conv is None:
        past_conv = jnp.zeros((b, 2, CONV_WIDTH), F32)
    q, ckv, kpe, convn, new_conv = _front(x, past_conv, tabs, w, ts=_tile(s, 512), s_valid=s_valid)

    if past_kv is None:
        ckv_all, kpe_all, sk_valid = ckv, kpe, s
    else:
        ckv_all = jnp.concatenate([past_kv, ckv[:, :s_valid]], axis=1)
        kpe_all = jnp.concatenate([past_kpe, kpe[:, :s_valid]], axis=1)
        sk_valid = ckv_all.shape[1]
    tk = 512 if sk_valid % 512 == 0 else 256
    pad = -sk_valid % tk
    if pad:
        ckv_all = jnp.pad(ckv_all, ((0, 0), (0, pad), (0, 0)))
        kpe_all = jnp.pad(kpe_all, ((0, 0), (0, pad), (0, 0)))
    k, vt = _kv_up(ckv_all, kpe_all, w, tk=tk)
    attn_t = _attention(q, k, vt, tq=_tile(s, 512), q_off=pos0, sk_valid=sk_valid)
    x1, hn = _merge(x, attn_t, convn, w, ts=_tile(s, 512))

    t = b * s
    hn2 = hn.reshape(t, d)
    pick_i, pick_j, pick_g = _route(hn2, w, tr=_tile(t, 512))
    y = _experts(hn2, x1.reshape(t, d), pick_i, pick_j, pick_g, w, g_final, tt=_tile(t, 256), ni=8)
    return y.reshape(b, s, d), ckv, kpe, new_conv


def kernel(x_prompt, x_sample, cache_kv_latent, cache_k_rope, state_conv, g_attn_norm, w_in, g_q, w_uq, g_kv, w_ukv, w_conv, g_mla_out, g_conv_out, w_out, g_ffn_norm, w_peer_q, peer_keys, peer_u, peer_v, g_final):
    depth = w_in.shape[0]
    assert depth == 1, "the final norm is fused into the last layer; one layer supported"
    past_len = cache_kv_latent.shape[2]
    s_dec = x_sample.shape[1]
    s_pad = -(-s_dec // LANES) * LANES
    hs = jnp.pad(x_sample, ((0, 0), (0, s_pad - s_dec), (0, 0)))
    gfin = g_final.reshape(1, -1).astype(F32)
    w = _prep_weights(0, g_attn_norm, w_in, g_q, w_uq, g_kv, w_ukv, w_conv, g_mla_out, g_conv_out,
                      w_out, g_ffn_norm, w_peer_q, peer_keys, peer_u, peer_v)
    yp, kv_p, kpe_p, conv_p = _layer(x_prompt, 0, x_prompt.shape[1], None, None, None, w, gfin)
    ys, kv_s, kpe_s, conv_s = _layer(hs, past_len, s_dec, cache_kv_latent[0], cache_k_rope[0],
                                     state_conv[0], w, gfin)
    return (yp, ys[:, :s_dec],
            kv_p[None], kpe_p[None], conv_p[None],
            kv_s[None, :, :s_dec], kpe_s[None, :, :s_dec], conv_s[None])
```

```python
import functools

import numpy as np
import jax
import jax.numpy as jnp
from jax import lax
from jax.experimental import pallas as pl
from jax.experimental.pallas import tpu as pltpu

F32 = jnp.float32
BF16 = jnp.bfloat16
I32 = jnp.int32
U32 = jnp.uint32

EPS = 1e-6
NEG_INF = -1e30
CHUNK_SHIFT = 6
HEADS = 8
QK_NOPE = 64
QK_ROPE = 32
V_DIM = 64
VT_ROWS = 80
Q_RANK = 384
KV_RANK = 256
CONV_WIDTH = 512
ROPE_THETA = 10000.0
HEAD_PAD = 128
ROPE_LO = QK_NOPE
ROPE_HALF = QK_ROPE // 2
PEER_HEADS = 8
N_KEYS = 128
PEER_TOPK = 16
LANES = 128
SUBLANES = 8
W_PITCH = 136

CQ0, CKV0, KPE0, UC0, BG0, CG0, ZW = 0, 384, 640, 768, 1280, 1792, 2304

CANDS = sorted(
    [(a, b) for a in range(PEER_TOPK) for b in range(PEER_TOPK) if (a + 1) * (b + 1) <= PEER_TOPK],
    key=lambda ab: ab[0] * PEER_TOPK + ab[1])
N_CAND = len(CANDS)
N_CAND_PAD = -(-N_CAND // SUBLANES) * SUBLANES


def _rms(x, g):
    ms = jnp.mean(x * x, axis=-1, keepdims=True)
    return x * lax.rsqrt(ms + EPS) * g


def _swap_rope_halves(v):
    lane = lax.broadcasted_iota(I32, v.shape, 1)
    return jnp.where(lane < ROPE_LO + ROPE_HALF,
                     pltpu.roll(v, LANES - ROPE_HALF, 1), pltpu.roll(v, ROPE_HALF, 1))


def _params(vmem_mb, n_axes):
    return pltpu.CompilerParams(dimension_semantics=("arbitrary",) * n_axes,
                                vmem_limit_bytes=vmem_mb * 1024 * 1024)


def _front_kernel(x_ref, win_ref, gattn_ref, gq_ref, wuq_ref, gkv_ref, wconv_ref, gconv_ref,
                  cq_ref, sq_ref, ck_ref, sk_ref, past_ref,
                  q_ref, ckv_ref, kpe_ref, convn_ref, newconv_ref, carry_ref,
                  *, ts, nc_tile, nc_row):
    si = pl.program_id(1)
    x = x_ref[0]
    h = _rms(x, gattn_ref[...]).astype(BF16)
    z = jnp.dot(h, win_ref[...], preferred_element_type=F32)

    cqn = _rms(z[:, CQ0:CKV0], gq_ref[...]).astype(BF16)
    q = jnp.dot(cqn, wuq_ref[...], preferred_element_type=F32)
    cq = cq_ref[...]
    sq = sq_ref[...]
    for hd in range(HEADS):
        qh = q[:, hd * HEAD_PAD:(hd + 1) * HEAD_PAD]
        q_ref[0, hd] = (qh * cq + _swap_rope_halves(qh) * sq).astype(BF16)

    ckv_ref[0] = _rms(z[:, CKV0:KPE0], gkv_ref[...])

    kp = z[:, KPE0:UC0]
    kp = kp * ck_ref[...] + _swap_rope_halves(kp) * sk_ref[...]
    kpe_ref[0] = kp[:, ROPE_LO:ROPE_LO + QK_ROPE]

    u = z[:, CG0:ZW] * z[:, UC0:BG0]

    @pl.when(si == 0)
    def _():
        carry_ref[0:2, :] = past_ref[0]

    prev2 = carry_ref[0:1, :]
    prev1 = carry_ref[1:2, :]
    row = lax.broadcasted_iota(I32, u.shape, 0)
    u1 = jnp.where(row == 0, prev1, pltpu.roll(u, 1, 0))
    u2 = jnp.where(row == 0, prev2, jnp.where(row == 1, prev1, pltpu.roll(u, 2, 0)))
    conv = u2 * wconv_ref[0:1, :] + u1 * wconv_ref[1:2, :] + u * wconv_ref[2:3, :]
    convn_ref[0] = _rms(z[:, BG0:CG0] * conv, gconv_ref[...]).astype(BF16)
    carry_ref[0:2, :] = u[ts - 2:ts, :]

    @pl.when(si == nc_tile)
    def _():
        newconv_ref[0] = u[nc_row:nc_row + 2, :]


def _front(x, past_conv, tabs, w, *, ts, s_valid):
    b, s, d = x.shape
    cq, sq, ck, sk = tabs
    full = lambda a: pl.BlockSpec(a.shape, lambda i, j: (0,) * a.ndim)
    tab = pl.BlockSpec((ts, LANES), lambda i, j: (j, 0))
    kern = functools.partial(_front_kernel, ts=ts, nc_tile=(s_valid - 2) // ts,
                             nc_row=(s_valid - 2) % ts)
    return pl.pallas_call(
        kern,
        grid=(b, s // ts),
        in_specs=[pl.BlockSpec((1, ts, d), lambda i, j: (i, j, 0)),
                  full(w["w_in"]), full(w["g_attn"]), full(w["g_q"]), full(w["w_uq"]),
                  full(w["g_kv"]), full(w["w_conv"]), full(w["g_conv"]),
                  tab, tab, tab, tab,
                  pl.BlockSpec((1, 2, CONV_WIDTH), lambda i, j: (i, 0, 0))],
        out_specs=[pl.BlockSpec((1, HEADS, ts, HEAD_PAD), lambda i, j: (i, 0, j, 0)),
                   pl.BlockSpec((1, ts, KV_RANK), lambda i, j: (i, j, 0)),
                   pl.BlockSpec((1, ts, QK_ROPE), lambda i, j: (i, j, 0)),
                   pl.BlockSpec((1, ts, CONV_WIDTH), lambda i, j: (i, j, 0)),
                   pl.BlockSpec((1, 2, CONV_WIDTH), lambda i, j: (i, 0, 0))],
        out_shape=[jax.ShapeDtypeStruct((b, HEADS, s, HEAD_PAD), BF16),
                   jax.ShapeDtypeStruct((b, s, KV_RANK), F32),
                   jax.ShapeDtypeStruct((b, s, QK_ROPE), F32),
                   jax.ShapeDtypeStruct((b, s, CONV_WIDTH), BF16),
                   jax.ShapeDtypeStruct((b, 2, CONV_WIDTH), F32)],
        scratch_shapes=[pltpu.VMEM((SUBLANES, CONV_WIDTH), F32)],
        compiler_params=_params(48, 2),
        name="front",
    )(x, w["w_in"], w["g_attn"], w["g_q"], w["w_uq"], w["g_kv"], w["w_conv"], w["g_conv"],
      cq, sq, ck, sk, past_conv)


def _kvup_kernel(ckv_ref, kpe_ref, wk_ref, wvt_ref, place_ref, k_ref, vt_ref):
    c = ckv_ref[0].astype(BF16)
    kn = jnp.dot(c, wk_ref[...], preferred_element_type=F32)
    kp = jnp.dot(kpe_ref[0].astype(BF16), place_ref[...], preferred_element_type=F32)
    for hd in range(HEADS):
        k_ref[0, hd] = (kn[:, hd * HEAD_PAD:(hd + 1) * HEAD_PAD] + kp).astype(BF16)
    vt = lax.dot_general(wvt_ref[...], c, (((1,), (1,)), ((), ())),
                         preferred_element_type=F32)
    ones_row = jnp.where(lax.broadcasted_iota(I32, (VT_ROWS - V_DIM, vt.shape[1]), 0) == 0,
                         1.0, 0.0).astype(BF16)
    for hd in range(HEADS):
        vt_ref[0, hd, 0, 0:V_DIM, :] = vt[hd * V_DIM:(hd + 1) * V_DIM, :].astype(BF16)
        vt_ref[0, hd, 0, V_DIM:VT_ROWS, :] = ones_row


def _kv_up(ckv, kpe, w, *, tk):
    b, sk, _ = ckv.shape
    nkb = sk // tk
    full = lambda a: pl.BlockSpec(a.shape, lambda i, j: (0,) * a.ndim)
    return pl.pallas_call(
        _kvup_kernel,
        grid=(b, nkb),
        in_specs=[pl.BlockSpec((1, tk, KV_RANK), lambda i, j: (i, j, 0)),
                  pl.BlockSpec((1, tk, QK_ROPE), lambda i, j: (i, j, 0)),
                  full(w["w_k"]), full(w["w_vt"]), full(w["place"])],
        out_specs=[pl.BlockSpec((1, HEADS, tk, HEAD_PAD), lambda i, j: (i, 0, j, 0)),
                   pl.BlockSpec((1, HEADS, 1, VT_ROWS, tk), lambda i, j: (i, 0, j, 0, 0))],
        out_shape=[jax.ShapeDtypeStruct((b, HEADS, sk, HEAD_PAD), BF16),
                   jax.ShapeDtypeStruct((b, HEADS, nkb, VT_ROWS, tk), BF16)],
        compiler_params=_params(32, 2),
        name="kv_up",
    )(ckv, kpe, w["w_k"], w["w_vt"], w["place"])


def _attn_kernel(q_ref, k_ref, vt_ref, o_ref, sa_ref, sb_ref, *, tq, tk, nkb, q_off, sk_valid):
    qi = pl.program_id(2)
    q = q_ref[0, 0]
    q0 = q_off + qi * tq
    qch = lax.shift_right_logical(q0 + lax.broadcasted_iota(I32, (1, tq), 1), CHUNK_SHIFT)
    last_visible = lax.shift_left(lax.shift_right_logical(q0 + tq - 1, CHUNK_SHIFT) + 1,
                                  CHUNK_SHIFT) - 1
    nb = jnp.minimum(nkb, last_visible // tk + 1)
    first_chunk_end = lax.shift_left(lax.shift_right_logical(q0, CHUNK_SHIFT) + 1, CHUNK_SHIFT)
    n_full = jnp.minimum(jnp.minimum(first_chunk_end // tk, sk_valid // tk), nb)

    def scores(kb):
        k = k_ref[0, 0, pl.ds(pl.multiple_of(kb * tk, tk), tk), :]
        return lax.dot_general(k, q, (((1,), (1,)), ((), ())), preferred_element_type=F32)

    def update(s_ref, kb, carry):
        m, acc = carry
        m_new = jnp.maximum(m, jnp.max(s_ref[...], axis=0, keepdims=True))
        p = jnp.exp2(s_ref[...] - m_new).astype(BF16)
        acc = jnp.exp2(m - m_new) * acc + jnp.dot(vt_ref[0, 0, kb], p,
                                                  preferred_element_type=F32)
        return m_new, acc

    def pair(j, carry):
        sb_ref[...] = scores(2 * j + 1)
        carry = update(sa_ref, 2 * j, carry)
        sa_ref[...] = scores(jnp.minimum(2 * j + 2, nkb - 1))
        return update(sb_ref, 2 * j + 1, carry)

    def masked_update(s_ref, kb, carry):
        kpos = kb * tk + lax.broadcasted_iota(I32, (tk, 1), 0)
        vis = (lax.shift_right_logical(kpos, CHUNK_SHIFT) <= qch) & (kpos < sk_valid)
        s_ref[...] = jnp.where(vis, s_ref[...], NEG_INF)
        return update(s_ref, kb, carry)

    n_pairs = n_full // 2
    base = 2 * n_pairs
    rest = nb - base

    def tail_one(_, carry):
        return masked_update(sa_ref, base, carry)

    def tail_two(_, carry):
        sb_ref[...] = scores(base + 1)
        carry = masked_update(sa_ref, base, carry)
        return masked_update(sb_ref, base + 1, carry)

    def tail_more(kb, carry):
        sa_ref[...] = scores(kb)
        return masked_update(sa_ref, kb, carry)

    sa_ref[...] = scores(0)
    carry = (jnp.full((1, tq), NEG_INF, F32), jnp.zeros((VT_ROWS, tq), F32))
    carry = lax.fori_loop(0, n_pairs, pair, carry)
    carry = lax.fori_loop(0, (rest == 1).astype(I32), tail_one, carry)
    carry = lax.fori_loop(0, (rest >= 2).astype(I32), tail_two, carry)
    _, acc = lax.fori_loop(base + 2, nb, tail_more, carry)
    o_ref[0, 0] = acc[:V_DIM] / acc[V_DIM:V_DIM + 1]


def _attention(q, k, vt, *, tq, q_off, sk_valid):
    b, _, s, _ = q.shape
    _, _, nkb, _, tk = vt.shape
    sk = k.shape[2]
    kern = functools.partial(_attn_kernel, tq=tq, tk=tk, nkb=nkb, q_off=q_off, sk_valid=sk_valid)
    return pl.pallas_call(
        kern,
        grid=(b, HEADS, s // tq),
        in_specs=[pl.BlockSpec((1, 1, tq, HEAD_PAD), lambda i, h, j: (i, h, j, 0)),
                  pl.BlockSpec((1, 1, sk, HEAD_PAD), lambda i, h, j: (i, h, 0, 0)),
                  pl.BlockSpec((1, 1, nkb, VT_ROWS, tk), lambda i, h, j: (i, h, 0, 0, 0))],
        out_specs=pl.BlockSpec((1, 1, V_DIM, tq), lambda i, h, j: (i, h, 0, j)),
        out_shape=jax.ShapeDtypeStruct((b, HEADS, V_DIM, s), F32),
        scratch_shapes=[pltpu.VMEM((tk, tq), F32), pltpu.VMEM((tk, tq), F32)],
        compiler_params=_params(32, 3),
        name="attn",
    )(q, k, vt)


def _merge_kernel(x_ref, at_ref, cn_ref, gm_ref, woa_ref, woc_ref, gf_ref, x1_ref, hn_ref, *, ts):
    a = at_ref[0].reshape(HEADS * V_DIM, ts).T
    an = _rms(a, gm_ref[...]).astype(BF16)
    mix = (jnp.dot(an, woa_ref[...], preferred_element_type=F32)
           + jnp.dot(cn_ref[0], woc_ref[...], preferred_element_type=F32))
    x1 = x_ref[0] + mix
    x1_ref[0] = x1
    hn_ref[0] = _rms(x1, gf_ref[...]).astype(BF16)


def _merge(x, attn_t, convn, w, *, ts):
    b, s, d = x.shape
    full = lambda a: pl.BlockSpec(a.shape, lambda i, j: (0,) * a.ndim)
    return pl.pallas_call(
        functools.partial(_merge_kernel, ts=ts),
        grid=(b, s // ts),
        in_specs=[pl.BlockSpec((1, ts, d), lambda i, j: (i, j, 0)),
                  pl.BlockSpec((1, HEADS, V_DIM, ts), lambda i, j: (i, 0, 0, j)),
                  pl.BlockSpec((1, ts, CONV_WIDTH), lambda i, j: (i, j, 0)),
                  full(w["g_mla"]), full(w["w_o_attn"]), full(w["w_o_conv"]), full(w["g_ffn"])],
        out_specs=[pl.BlockSpec((1, ts, d), lambda i, j: (i, j, 0)),
                   pl.BlockSpec((1, ts, d), lambda i, j: (i, j, 0))],
        out_shape=[jax.ShapeDtypeStruct((b, s, d), F32), jax.ShapeDtypeStruct((b, s, d), BF16)],
        compiler_params=_params(32, 2),
        name="merge",
    )(x, attn_t, convn, w["g_mla"], w["w_o_attn"], w["w_o_conv"], w["g_ffn"])


def _route_kernel(hn_ref, wq_ref, keys_ref, i_out, j_out, g_out,
                  qt_ref, sv_ref, si_ref, cand_ref, ce_ref, eb_ref, gb_ref, *, tr):
    n_half = 2 * PEER_HEADS
    qt = lax.dot_general(wq_ref[...], hn_ref[...], (((1,), (1,)), ((), ())),
                         preferred_element_type=F32)
    qt_ref[...] = qt.astype(BF16).reshape(n_half, N_KEYS, tr)

    def half_body(hc, _):
        s = jnp.dot(keys_ref[hc], qt_ref[hc], preferred_element_type=F32)
        key_id = lax.broadcasted_iota(I32, (N_KEYS, tr), 0)
        for k in range(PEER_TOPK):
            m = jnp.max(s, axis=0, keepdims=True)
            idx = jnp.min(jnp.where(s == m, key_id, N_KEYS), axis=0, keepdims=True)
            sv_ref[hc, k:k + 1, :] = m
            si_ref[hc, k:k + 1, :] = idx
            s = jnp.where(key_id == idx, -jnp.inf, s)
        return 0

    lax.fori_loop(0, n_half, half_body, 0)

    cand_ref[N_CAND:N_CAND_PAD, :] = jnp.full((N_CAND_PAD - N_CAND, tr), -jnp.inf, F32)
    ce_ref[N_CAND:N_CAND_PAD, :] = jnp.zeros((N_CAND_PAD - N_CAND, tr), I32)

    def head_body(hd, _):
        for c, (a, b) in enumerate(CANDS):
            cand_ref[c:c + 1, :] = sv_ref[2 * hd, a:a + 1, :] + sv_ref[2 * hd + 1, b:b + 1, :]
            ce_ref[c:c + 1, :] = (si_ref[2 * hd, a:a + 1, :] * N_KEYS
                                  + si_ref[2 * hd + 1, b:b + 1, :])
        cand = cand_ref[...]
        ce = ce_ref[...]
        slot = lax.broadcasted_iota(I32, (N_CAND_PAD, tr), 0)
        for k in range(PEER_TOPK):
            m = jnp.max(cand, axis=0, keepdims=True)
            idx = jnp.min(jnp.where(cand == m, slot, N_CAND_PAD), axis=0, keepdims=True)
            sel = slot == idx
            gb_ref[hd, k:k + 1, :] = m
            eb_ref[hd, k:k + 1, :] = jnp.max(jnp.where(sel, ce, -1), axis=0, keepdims=True)
            cand = jnp.where(sel, -jnp.inf, cand)
        best = gb_ref[hd]
        e = jnp.exp(best - best[0:1, :])
        gb_ref[hd] = e / jnp.sum(e, axis=0, keepdims=True)
        return 0

    lax.fori_loop(0, PEER_HEADS, head_body, 0)

    n_pick = PEER_HEADS * PEER_TOPK
    experts = eb_ref[...].reshape(n_pick, tr).T
    i_out[...] = lax.shift_right_logical(experts, 7)
    j_out[...] = experts & (N_KEYS - 1)
    g_out[...] = gb_ref[...].reshape(n_pick, tr).T


def _route(hn, w, *, tr):
    t, d = hn.shape
    n_half = 2 * PEER_HEADS
    n_pick = PEER_HEADS * PEER_TOPK
    full = lambda a: pl.BlockSpec(a.shape, lambda i: (0,) * a.ndim)
    pick = pl.BlockSpec((tr, n_pick), lambda i: (i, 0))
    return pl.pallas_call(
        functools.partial(_route_kernel, tr=tr),
        grid=(t // tr,),
        in_specs=[pl.BlockSpec((tr, d), lambda i: (i, 0)), full(w["w_pq_t"]), full(w["keys"])],
        out_specs=[pick, pick, pick],
        out_shape=[jax.ShapeDtypeStruct((t, n_pick), I32), jax.ShapeDtypeStruct((t, n_pick), I32),
                   jax.ShapeDtypeStruct((t, n_pick), F32)],
        scratch_shapes=[pltpu.VMEM((n_half, N_KEYS, tr), BF16),
                        pltpu.VMEM((n_half, PEER_TOPK, tr), F32),
                        pltpu.VMEM((n_half, PEER_TOPK, tr), I32),
                        pltpu.VMEM((N_CAND_PAD, tr), F32),
                        pltpu.VMEM((N_CAND_PAD, tr), I32),
                        pltpu.VMEM((PEER_HEADS, PEER_TOPK, tr), I32),
                        pltpu.VMEM((PEER_HEADS, PEER_TOPK, tr), F32)],
        compiler_params=_params(32, 1),
        name="route",
    )(hn, w["w_pq_t"], w["keys"])


def _expert_kernel(hn_ref, x1_ref, i_ref, j_ref, g_ref, u_ref, v_ref, gfin_ref, y_ref,
                   w_ref, acc_ref, *, tt, ni, n_et):
    ne = pl.program_id(1)

    @pl.when(ne == 0)
    def _():
        acc_ref[...] = jnp.zeros_like(acc_ref)
        key_id = lax.broadcasted_iota(I32, (N_KEYS, N_KEYS), 0)
        zero = jnp.zeros((N_KEYS, N_KEYS), BF16)

        def one_hots(t):
            irow = i_ref[pl.ds(t, 1), :]
            jrow = j_ref[pl.ds(t, 1), :]
            grow = g_ref[pl.ds(t, 1), :]
            at = jnp.where(key_id == irow, grow, 0.0).astype(BF16)
            bt = jnp.where(key_id == jrow, 1.0, 0.0).astype(BF16)
            return at, bt

        def token_pair(tp, _):
            at0, bt0 = one_hots(tp)
            at1, bt1 = one_hots(tp + tt // 2)
            lhs = jnp.concatenate([at0, at1], axis=1)
            rhs = jnp.concatenate([jnp.concatenate([bt0, zero], axis=1),
                                   jnp.concatenate([zero, bt1], axis=1)], axis=0)
            w = lax.dot_general(lhs, rhs, (((1,), (1,)), ((), ())), preferred_element_type=F32)
            lo = lax.bitcast_convert_type(w[:, :N_KEYS].astype(BF16).astype(F32), U32)
            hi = lax.bitcast_convert_type(w[:, N_KEYS:].astype(BF16).astype(F32), U32)
            row0 = pl.multiple_of(tp * W_PITCH, SUBLANES)
            w_ref[pl.ds(row0, N_KEYS), :] = hi | lax.shift_right_logical(lo, jnp.uint32(16))
            return 0

        lax.fori_loop(0, tt // 2, token_pair, 0, unroll=16)

    a = lax.dot_general(hn_ref[...], u_ref[...], (((1,), (1,)), ((), ())),
                        preferred_element_type=F32)
    act = 0.5 * a * (1.0 + lax.erf(a * np.float32(np.sqrt(0.5))))

    def gate_columns(ii):
        words = w_ref[pl.ds(ne * ni + ii, tt // 2, stride=W_PITCH), :]
        first = lax.bitcast_convert_type(lax.shift_left(words, jnp.uint32(16)), F32)
        second = lax.bitcast_convert_type(words & jnp.uint32(0xFFFF0000), F32)
        return jnp.concatenate([first, second], axis=0)

    w = jnp.concatenate([gate_columns(ii) for ii in range(ni)], axis=1)
    acc_ref[...] += jnp.dot((w * act).astype(BF16), v_ref[...], preferred_element_type=F32)

    @pl.when(ne == n_et - 1)
    def _():
        y_ref[...] = _rms(x1_ref[...] + acc_ref[...], gfin_ref[...])


def _experts(hn, x1, pick_i, pick_j, pick_g, w, g_final, *, tt, ni):
    t, d = hn.shape
    n_exp = w["u"].shape[0]
    et = ni * N_KEYS
    n_et = n_exp // et
    n_pick = PEER_HEADS * PEER_TOPK
    tok = lambda width: pl.BlockSpec((tt, width), lambda i, e: (i, 0))
    return pl.pallas_call(
        functools.partial(_expert_kernel, tt=tt, ni=ni, n_et=n_et),
        grid=(t // tt, n_et),
        in_specs=[tok(d), tok(d), tok(n_pick), tok(n_pick), tok(n_pick),
                  pl.BlockSpec((et, d), lambda i, e: (e, 0)),
                  pl.BlockSpec((et, d), lambda i, e: (e, 0)),
                  pl.BlockSpec((1, d), lambda i, e: (0, 0))],
        out_specs=tok(d),
        out_shape=jax.ShapeDtypeStruct((t, d), F32),
        scratch_shapes=[pltpu.VMEM((tt // 2 * W_PITCH, N_KEYS), U32), pltpu.VMEM((tt, d), F32)],
        compiler_params=_params(56, 2),
        name="experts",
    )(hn, x1, pick_i, pick_j, pick_g, w["u"], w["v"], g_final)


def _rope_tables(pos, scale):
    freqs = ROPE_THETA ** (-jnp.arange(ROPE_HALF, dtype=F32) / ROPE_HALF)
    ang = pos.astype(F32)[:, None] * freqs[None, :]
    cos, sin = jnp.cos(ang), jnp.sin(ang)
    n = pos.shape[0]
    ones = jnp.ones((n, QK_NOPE), F32)
    tail = jnp.ones((n, HEAD_PAD - QK_NOPE - QK_ROPE), F32)
    cos_tab = jnp.concatenate([ones, cos, cos, tail], axis=1)
    sin_tab = jnp.concatenate([0 * ones, -sin, sin, 0 * tail], axis=1)
    return cos_tab * scale, sin_tab * scale, cos_tab, sin_tab


def _prep_weights(l, g_attn_norm, w_in, g_q, w_uq, g_kv, w_ukv, w_conv, g_mla_out, g_conv_out,
                  w_out, g_ffn_norm, w_peer_q, peer_keys, peer_u, peer_v):
    d = w_in.shape[1]
    zeros = lambda n: jnp.zeros((d, n), w_in.dtype)
    kpe_end = KV_RANK + Q_RANK + QK_ROPE
    w_in_r = jnp.concatenate(
        [w_in[l][:, :CKV0 + KV_RANK], zeros(ROPE_LO), w_in[l][:, CKV0 + KV_RANK:kpe_end],
         zeros(HEAD_PAD - ROPE_LO - QK_ROPE), w_in[l][:, kpe_end:]], axis=1)
    w_uq_r = jnp.pad(w_uq[l].reshape(Q_RANK, HEADS, QK_NOPE + QK_ROPE),
                     ((0, 0), (0, 0), (0, HEAD_PAD - QK_NOPE - QK_ROPE)))
    kv3 = w_ukv[l].reshape(KV_RANK, HEADS, QK_NOPE + V_DIM)
    w_k = jnp.pad(kv3[:, :, :QK_NOPE], ((0, 0), (0, 0), (0, HEAD_PAD - QK_NOPE)))
    w_vt = kv3[:, :, QK_NOPE:].reshape(KV_RANK, HEADS * V_DIM).T
    place = jnp.zeros((QK_ROPE, HEAD_PAD), F32).at[
        jnp.arange(QK_ROPE), ROPE_LO + jnp.arange(QK_ROPE)].set(1.0)
    row = lambda g: g[l].reshape(1, -1).astype(F32)
    mla_w = HEADS * V_DIM
    return {
        "w_in": w_in_r.astype(BF16), "g_attn": row(g_attn_norm), "g_q": row(g_q),
        "w_uq": w_uq_r.reshape(Q_RANK, HEADS * HEAD_PAD).astype(BF16), "g_kv": row(g_kv),
        "w_conv": w_conv[l].astype(F32), "g_conv": row(g_conv_out),
        "w_k": w_k.reshape(KV_RANK, HEADS * HEAD_PAD).astype(BF16), "w_vt": w_vt.astype(BF16),
        "place": place.astype(BF16),
        "g_mla": row(g_mla_out), "w_o_attn": w_out[l][:mla_w].astype(BF16),
        "w_o_conv": w_out[l][mla_w:].astype(BF16), "g_ffn": row(g_ffn_norm),
        "w_pq_t": w_peer_q[l].T.astype(BF16),
        "keys": peer_keys[l].reshape(2 * PEER_HEADS, N_KEYS, -1).astype(BF16),
        "u": peer_u[l].astype(BF16), "v": peer_v[l].astype(BF16),
    }


def _tile(n, pref):
    t = min(n, pref)
    assert n % t == 0, (n, pref)
    return t


def _layer(x, pos0, s_valid, past_kv, past_kpe, past_conv, w, g_final):
    b, s, d = x.shape
    scale = float(QK_NOPE + QK_ROPE) ** -0.5 * float(np.log2(np.e))
    tabs = _rope_tables(pos0 + jnp.arange(s, dtype=jnp.int32), scale)
    if past_conv is None:
        past_conv = jnp.zeros((b, 2, CONV_WIDTH), F32)
    q, ckv, kpe, convn, new_conv = _front(x, past_conv, tabs, w, ts=_tile(s, 512), s_valid=s_valid)

    if past_kv is None:
        ckv_all, kpe_all, sk_valid = ckv, kpe, s
    else:
        ckv_all = jnp.concatenate([past_kv, ckv[:, :s_valid]], axis=1)
        kpe_all = jnp.concatenate([past_kpe, kpe[:, :s_valid]], axis=1)
        sk_valid = ckv_all.shape[1]
    tk = 512 if sk_valid % 512 == 0 else 256
    pad = -sk_valid % tk
    if pad:
        ckv_all = jnp.pad(ckv_all, ((0, 0), (0, pad), (0, 0)))
        kpe_all = jnp.pad(kpe_all, ((0, 0), (0, pad), (0, 0)))
    k, vt = _kv_up(ckv_all, kpe_all, w, tk=tk)
    attn_t = _attention(q, k, vt, tq=_tile(s, 512), q_off=pos0, sk_valid=sk_valid)
    x1, hn = _merge(x, attn_t, convn, w, ts=_tile(s, 512))

    t = b * s
    hn2 = hn.reshape(t, d)
    pick_i, pick_j, pick_g = _route(hn2, w, tr=_tile(t, 512))
    y = _experts(hn2, x1.reshape(t, d), pick_i, pick_j, pick_g, w, g_final, tt=_tile(t, 512), ni=8)
    return y.reshape(b, s, d), ckv, kpe, new_conv


def kernel(x_prompt, x_sample, cache_kv_latent, cache_k_rope, state_conv, g_attn_norm, w_in, g_q, w_uq, g_kv, w_ukv, w_conv, g_mla_out, g_conv_out, w_out, g_ffn_norm, w_peer_q, peer_keys, peer_u, peer_v, g_final):
    depth = w_in.shape[0]
    assert depth == 1, "the final norm is fused into the last layer; one layer supported"
    past_len = cache_kv_latent.shape[2]
    s_dec = x_sample.shape[1]
    s_pad = -(-s_dec // LANES) * LANES
    hs = jnp.pad(x_sample, ((0, 0), (0, s_pad - s_dec), (0, 0)))
    gfin = g_final.reshape(1, -1).astype(F32)
    w = _prep_weights(0, g_attn_norm, w_in, g_q, w_uq, g_kv, w_ukv, w_conv, g_mla_out, g_conv_out,
                      w_out, g_ffn_norm, w_peer_q, peer_keys, peer_u, peer_v)
    yp, kv_p, kpe_p, conv_p = _layer(x_prompt, 0, x_prompt.shape[1], None, None, None, w, gfin)
    ys, kv_s, kpe_s, conv_s = _layer(hs, past_len, s_dec, cache_kv_latent[0], cache_k_rope[0],
                                     state_conv[0], w, gfin)
    return (yp, ys[:, :s_dec],
            kv_p[None], kpe_p[None], conv_p[None],
            kv_s[None, :, :s_dec], kpe_s[None, :, :s_dec], conv_s[None])
```

```python
import functools

import numpy as np
import jax
import jax.numpy as jnp
from jax import lax
from jax.experimental import pallas as pl
from jax.experimental.pallas import tpu as pltpu

F32 = jnp.float32
BF16 = jnp.bfloat16
I32 = jnp.int32
U32 = jnp.uint32

EPS = 1e-6
NEG_INF = -1e30
CHUNK_SHIFT = 6
HEADS = 8
QK_NOPE = 64
QK_ROPE = 32
V_DIM = 64
VT_ROWS = 80
Q_RANK = 384
KV_RANK = 256
CONV_WIDTH = 512
ROPE_THETA = 10000.0
HEAD_PAD = 128
ROPE_LO = QK_NOPE
ROPE_HALF = QK_ROPE // 2
PEER_HEADS = 8
N_KEYS = 128
PEER_TOPK = 16
LANES = 128
SUBLANES = 8
W_PITCH = 136

CQ0, CKV0, KPE0, UC0, BG0, CG0, ZW = 0, 384, 640, 768, 1280, 1792, 2304

CANDS = sorted(
    [(a, b) for a in range(PEER_TOPK) for b in range(PEER_TOPK) if (a + 1) * (b + 1) <= PEER_TOPK],
    key=lambda ab: ab[0] * PEER_TOPK + ab[1])
N_CAND = len(CANDS)
N_CAND_PAD = -(-N_CAND // SUBLANES) * SUBLANES


def _rms(x, g):
    ms = jnp.mean(x * x, axis=-1, keepdims=True)
    return x * lax.rsqrt(ms + EPS) * g


def _swap_rope_halves(v):
    lane = lax.broadcasted_iota(I32, v.shape, 1)
    return jnp.where(lane < ROPE_LO + ROPE_HALF,
                     pltpu.roll(v, LANES - ROPE_HALF, 1), pltpu.roll(v, ROPE_HALF, 1))


def _params(vmem_mb, n_axes):
    return pltpu.CompilerParams(dimension_semantics=("arbitrary",) * n_axes,
                                vmem_limit_bytes=vmem_mb * 1024 * 1024)


def _front_kernel(x_ref, win_ref, gattn_ref, gq_ref, wuq_ref, gkv_ref, wconv_ref, gconv_ref,
                  cq_ref, sq_ref, ck_ref, sk_ref, past_ref,
                  q_ref, ckv_ref, kpe_ref, convn_ref, newconv_ref, carry_ref,
                  *, ts, nc_tile, nc_row):
    si = pl.program_id(1)
    x = x_ref[0]
    h = _rms(x, gattn_ref[...]).astype(BF16)
    z = jnp.dot(h, win_ref[...], preferred_element_type=F32)

    cqn = _rms(z[:, CQ0:CKV0], gq_ref[...]).astype(BF16)
    q = jnp.dot(cqn, wuq_ref[...], preferred_element_type=F32)
    cq = cq_ref[...]
    sq = sq_ref[...]
    for hd in range(HEADS):
        qh = q[:, hd * HEAD_PAD:(hd + 1) * HEAD_PAD]
        q_ref[0, hd] = (qh * cq + _swap_rope_halves(qh) * sq).astype(BF16)

    ckv_ref[0] = _rms(z[:, CKV0:KPE0], gkv_ref[...])

    kp = z[:, KPE0:UC0]
    kp = kp * ck_ref[...] + _swap_rope_halves(kp) * sk_ref[...]
    kpe_ref[0] = kp[:, ROPE_LO:ROPE_LO + QK_ROPE]

    u = z[:, CG0:ZW] * z[:, UC0:BG0]

    @pl.when(si == 0)
    def _():
        carry_ref[0:2, :] = past_ref[0]

    prev2 = carry_ref[0:1, :]
    prev1 = carry_ref[1:2, :]
    row = lax.broadcasted_iota(I32, u.shape, 0)
    u1 = jnp.where(row == 0, prev1, pltpu.roll(u, 1, 0))
    u2 = jnp.where(row == 0, prev2, jnp.where(row == 1, prev1, pltpu.roll(u, 2, 0)))
    conv = u2 * wconv_ref[0:1, :] + u1 * wconv_ref[1:2, :] + u * wconv_ref[2:3, :]
    convn_ref[0] = _rms(z[:, BG0:CG0] * conv, gconv_ref[...]).astype(BF16)
    carry_ref[0:2, :] = u[ts - 2:ts, :]

    @pl.when(si == nc_tile)
    def _():
        newconv_ref[0] = u[nc_row:nc_row + 2, :]


def _front(x, past_conv, tabs, w, *, ts, s_valid):
    b, s, d = x.shape
    cq, sq, ck, sk = tabs
    full = lambda a: pl.BlockSpec(a.shape, lambda i, j: (0,) * a.ndim)
    tab = pl.BlockSpec((ts, LANES), lambda i, j: (j, 0))
    kern = functools.partial(_front_kernel, ts=ts, nc_tile=(s_valid - 2) // ts,
                             nc_row=(s_valid - 2) % ts)
    return pl.pallas_call(
        kern,
        grid=(b, s // ts),
        in_specs=[pl.BlockSpec((1, ts, d), lambda i, j: (i, j, 0)),
                  full(w["w_in"]), full(w["g_attn"]), full(w["g_q"]), full(w["w_uq"]),
                  full(w["g_kv"]), full(w["w_conv"]), full(w["g_conv"]),
                  tab, tab, tab, tab,
                  pl.BlockSpec((1, 2, CONV_WIDTH), lambda i, j: (i, 0, 0))],
        out_specs=[pl.BlockSpec((1, HEADS, ts, HEAD_PAD), lambda i, j: (i, 0, j, 0)),
                   pl.BlockSpec((1, ts, KV_RANK), lambda i, j: (i, j, 0)),
                   pl.BlockSpec((1, ts, QK_ROPE), lambda i, j: (i, j, 0)),
                   pl.BlockSpec((1, ts, CONV_WIDTH), lambda i, j: (i, j, 0)),
                   pl.BlockSpec((1, 2, CONV_WIDTH), lambda i, j: (i, 0, 0))],
        out_shape=[jax.ShapeDtypeStruct((b, HEADS, s, HEAD_PAD), BF16),
                   jax.ShapeDtypeStruct((b, s, KV_RANK), F32),
                   jax.ShapeDtypeStruct((b, s, QK_ROPE), F32),
                   jax.ShapeDtypeStruct((b, s, CONV_WIDTH), BF16),
                   jax.ShapeDtypeStruct((b, 2, CONV_WIDTH), F32)],
        scratch_shapes=[pltpu.VMEM((SUBLANES, CONV_WIDTH), F32)],
        compiler_params=_params(48, 2),
        name="front",
    )(x, w["w_in"], w["g_attn"], w["g_q"], w["w_uq"], w["g_kv"], w["w_conv"], w["g_conv"],
      cq, sq, ck, sk, past_conv)


def _kvup_kernel(ckv_ref, kpe_ref, wk_ref, wvt_ref, place_ref, k_ref, vt_ref):
    c = ckv_ref[0].astype(BF16)
    kn = jnp.dot(c, wk_ref[...], preferred_element_type=F32)
    kp = jnp.dot(kpe_ref[0].astype(BF16), place_ref[...], preferred_element_type=F32)
    for hd in range(HEADS):
        k_ref[0, hd] = (kn[:, hd * HEAD_PAD:(hd + 1) * HEAD_PAD] + kp).astype(BF16)
    vt = lax.dot_general(wvt_ref[...], c, (((1,), (1,)), ((), ())),
                         preferred_element_type=F32)
    ones_row = jnp.where(lax.broadcasted_iota(I32, (VT_ROWS - V_DIM, vt.shape[1]), 0) == 0,
                         1.0, 0.0).astype(BF16)
    for hd in range(HEADS):
        vt_ref[0, hd, 0, 0:V_DIM, :] = vt[hd * V_DIM:(hd + 1) * V_DIM, :].astype(BF16)
        vt_ref[0, hd, 0, V_DIM:VT_ROWS, :] = ones_row


def _kv_up(ckv, kpe, w, *, tk):
    b, sk, _ = ckv.shape
    nkb = sk // tk
    full = lambda a: pl.BlockSpec(a.shape, lambda i, j: (0,) * a.ndim)
    return pl.pallas_call(
        _kvup_kernel,
        grid=(b, nkb),
        in_specs=[pl.BlockSpec((1, tk, KV_RANK), lambda i, j: (i, j, 0)),
                  pl.BlockSpec((1, tk, QK_ROPE), lambda i, j: (i, j, 0)),
                  full(w["w_k"]), full(w["w_vt"]), full(w["place"])],
        out_specs=[pl.BlockSpec((1, HEADS, tk, HEAD_PAD), lambda i, j: (i, 0, j, 0)),
                   pl.BlockSpec((1, HEADS, 1, VT_ROWS, tk), lambda i, j: (i, 0, j, 0, 0))],
        out_shape=[jax.ShapeDtypeStruct((b, HEADS, sk, HEAD_PAD), BF16),
                   jax.ShapeDtypeStruct((b, HEADS, nkb, VT_ROWS, tk), BF16)],
        compiler_params=_params(32, 2),
        name="kv_up",
    )(ckv, kpe, w["w_k"], w["w_vt"], w["place"])


def _attn_kernel(q_ref, k_ref, vt_ref, o_ref, sa_ref, sb_ref, *, tq, tk, nkb, q_off, sk_valid):
    qi = pl.program_id(2)
    q = q_ref[0, 0]
    q0 = q_off + qi * tq
    qch = lax.shift_right_logical(q0 + lax.broadcasted_iota(I32, (1, tq), 1), CHUNK_SHIFT)
    last_visible = lax.shift_left(lax.shift_right_logical(q0 + tq - 1, CHUNK_SHIFT) + 1,
                                  CHUNK_SHIFT) - 1
    nb = jnp.minimum(nkb, last_visible // tk + 1)
    first_chunk_end = lax.shift_left(lax.shift_right_logical(q0, CHUNK_SHIFT) + 1, CHUNK_SHIFT)
    n_full = jnp.minimum(jnp.minimum(first_chunk_end // tk, sk_valid // tk), nb)

    def scores(kb):
        k = k_ref[0, 0, pl.ds(pl.multiple_of(kb * tk, tk), tk), :]
        return lax.dot_general(k, q, (((1,), (1,)), ((), ())), preferred_element_type=F32)

    def update(s_ref, kb, carry):
        m, acc = carry
        m_new = jnp.maximum(m, jnp.max(s_ref[...], axis=0, keepdims=True))
        p = jnp.exp2(s_ref[...] - m_new).astype(BF16)
        acc = jnp.exp2(m - m_new) * acc + jnp.dot(vt_ref[0, 0, kb], p,
                                                  preferred_element_type=F32)
        return m_new, acc

    def pair(j, carry):
        sb_ref[...] = scores(2 * j + 1)
        carry = update(sa_ref, 2 * j, carry)
        sa_ref[...] = scores(jnp.minimum(2 * j + 2, nkb - 1))
        return update(sb_ref, 2 * j + 1, carry)

    def masked_update(s_ref, kb, carry):
        kpos = kb * tk + lax.broadcasted_iota(I32, (tk, 1), 0)
        vis = (lax.shift_right_logical(kpos, CHUNK_SHIFT) <= qch) & (kpos < sk_valid)
        s_ref[...] = jnp.where(vis, s_ref[...], NEG_INF)
        return update(s_ref, kb, carry)

    n_pairs = n_full // 2
    base = 2 * n_pairs
    rest = nb - base

    def tail_one(_, carry):
        return masked_update(sa_ref, base, carry)

    def tail_two(_, carry):
        sb_ref[...] = scores(base + 1)
        carry = masked_update(sa_ref, base, carry)
        return masked_update(sb_ref, base + 1, carry)

    def tail_more(kb, carry):
        sa_ref[...] = scores(kb)
        return masked_update(sa_ref, kb, carry)

    sa_ref[...] = scores(0)
    carry = (jnp.full((1, tq), NEG_INF, F32), jnp.zeros((VT_ROWS, tq), F32))
    carry = lax.fori_loop(0, n_pairs, pair, carry)
    carry = lax.fori_loop(0, (rest == 1).astype(I32), tail_one, carry)
    carry = lax.fori_loop(0, (rest >= 2).astype(I32), tail_two, carry)
    _, acc = lax.fori_loop(base + 2, nb, tail_more, carry)
    o_ref[0, 0] = acc[:V_DIM] / acc[V_DIM:V_DIM + 1]


def _attention(q, k, vt, *, tq, q_off, sk_valid):
    b, _, s, _ = q.shape
    _, _, nkb, _, tk = vt.shape
    sk = k.shape[2]
    kern = functools.partial(_attn_kernel, tq=tq, tk=tk, nkb=nkb, q_off=q_off, sk_valid=sk_valid)
    return pl.pallas_call(
        kern,
        grid=(b, HEADS, s // tq),
        in_specs=[pl.BlockSpec((1, 1, tq, HEAD_PAD), lambda i, h, j: (i, h, j, 0)),
                  pl.BlockSpec((1, 1, sk, HEAD_PAD), lambda i, h, j: (i, h, 0, 0)),
                  pl.BlockSpec((1, 1, nkb, VT_ROWS, tk), lambda i, h, j: (i, h, 0, 0, 0))],
        out_specs=pl.BlockSpec((1, 1, V_DIM, tq), lambda i, h, j: (i, h, 0, j)),
        out_shape=jax.ShapeDtypeStruct((b, HEADS, V_DIM, s), F32),
        scratch_shapes=[pltpu.VMEM((tk, tq), F32), pltpu.VMEM((tk, tq), F32)],
        compiler_params=_params(32, 3),
        name="attn",
    )(q, k, vt)


def _merge_kernel(x_ref, at_ref, cn_ref, gm_ref, woa_ref, woc_ref, gf_ref, x1_ref, hn_ref, *, ts):
    a = at_ref[0].reshape(HEADS * V_DIM, ts).T
    an = _rms(a, gm_ref[...]).astype(BF16)
    mix = (jnp.dot(an, woa_ref[...], preferred_element_type=F32)
           + jnp.dot(cn_ref[0], woc_ref[...], preferred_element_type=F32))
    x1 = x_ref[0] + mix
    x1_ref[0] = x1
    hn_ref[0] = _rms(x1, gf_ref[...]).astype(BF16)


def _merge(x, attn_t, convn, w, *, ts):
    b, s, d = x.shape
    full = lambda a: pl.BlockSpec(a.shape, lambda i, j: (0,) * a.ndim)
    return pl.pallas_call(
        functools.partial(_merge_kernel, ts=ts),
        grid=(b, s // ts),
        in_specs=[pl.BlockSpec((1, ts, d), lambda i, j: (i, j, 0)),
                  pl.BlockSpec((1, HEADS, V_DIM, ts), lambda i, j: (i, 0, 0, j)),
                  pl.BlockSpec((1, ts, CONV_WIDTH), lambda i, j: (i, j, 0)),
                  full(w["g_mla"]), full(w["w_o_attn"]), full(w["w_o_conv"]), full(w["g_ffn"])],
        out_specs=[pl.BlockSpec((1, ts, d), lambda i, j: (i, j, 0)),
                   pl.BlockSpec((1, ts, d), lambda i, j: (i, j, 0))],
        out_shape=[jax.ShapeDtypeStruct((b, s, d), F32), jax.ShapeDtypeStruct((b, s, d), BF16)],
        compiler_params=_params(32, 2),
        name="merge",
    )(x, attn_t, convn, w["g_mla"], w["w_o_attn"], w["w_o_conv"], w["g_ffn"])


def _route_kernel(hn_ref, wq_ref, keys_ref, i_out, j_out, g_out,
                  qt_ref, sv_ref, si_ref, cand_ref, ce_ref, eb_ref, gb_ref, *, tr):
    n_half = 2 * PEER_HEADS
    qt = lax.dot_general(wq_ref[...], hn_ref[...], (((1,), (1,)), ((), ())),
                         preferred_element_type=F32)
    qt_ref[...] = qt.astype(BF16).reshape(n_half, N_KEYS, tr)

    def half_body(hc, _):
        s = jnp.dot(keys_ref[hc], qt_ref[hc], preferred_element_type=F32)
        key_id = lax.broadcasted_iota(I32, (N_KEYS, tr), 0).astype(F32)
        for k in range(PEER_TOPK):
            m = jnp.max(s, axis=0, keepdims=True)
            idx = jnp.min(jnp.where(s == m, key_id, float(N_KEYS)), axis=0, keepdims=True)
            sv_ref[hc, k:k + 1, :] = m
            si_ref[hc, k:k + 1, :] = idx
            s = jnp.where(key_id == idx, -jnp.inf, s)
        return 0

    lax.fori_loop(0, n_half, half_body, 0, unroll=2)

    cand_ref[N_CAND:N_CAND_PAD, :] = jnp.full((N_CAND_PAD - N_CAND, tr), -jnp.inf, F32)
    ce_ref[N_CAND:N_CAND_PAD, :] = jnp.zeros((N_CAND_PAD - N_CAND, tr), F32)

    def head_body(hd, _):
        for c, (a, b) in enumerate(CANDS):
            cand_ref[c:c + 1, :] = sv_ref[2 * hd, a:a + 1, :] + sv_ref[2 * hd + 1, b:b + 1, :]
            ce_ref[c:c + 1, :] = (si_ref[2 * hd, a:a + 1, :] * float(N_KEYS)
                                  + si_ref[2 * hd + 1, b:b + 1, :])
        cand = cand_ref[...]
        ce = ce_ref[...]
        slot = lax.broadcasted_iota(I32, (N_CAND_PAD, tr), 0).astype(F32)
        for k in range(PEER_TOPK):
            m = jnp.max(cand, axis=0, keepdims=True)
            idx = jnp.min(jnp.where(cand == m, slot, float(N_CAND_PAD)), axis=0, keepdims=True)
            sel = slot == idx
            gb_ref[hd, k:k + 1, :] = m
            eb_ref[hd, k:k + 1, :] = jnp.max(jnp.where(sel, ce, -1.0), axis=0, keepdims=True)
            cand = jnp.where(sel, -jnp.inf, cand)
        best = gb_ref[hd]
        e = jnp.exp(best - best[0:1, :])
        gb_ref[hd] = e / jnp.sum(e, axis=0, keepdims=True)
        return 0

    lax.fori_loop(0, PEER_HEADS, head_body, 0)

    n_pick = PEER_HEADS * PEER_TOPK
    experts = eb_ref[...].reshape(n_pick, tr).T.astype(I32)
    i_out[...] = lax.shift_right_logical(experts, 7)
    j_out[...] = experts & (N_KEYS - 1)
    g_out[...] = gb_ref[...].reshape(n_pick, tr).T


def _route(hn, w, *, tr):
    t, d = hn.shape
    n_half = 2 * PEER_HEADS
    n_pick = PEER_HEADS * PEER_TOPK
    full = lambda a: pl.BlockSpec(a.shape, lambda i: (0,) * a.ndim)
    pick = pl.BlockSpec((tr, n_pick), lambda i: (i, 0))
    return pl.pallas_call(
        functools.partial(_route_kernel, tr=tr),
        grid=(t // tr,),
        in_specs=[pl.BlockSpec((tr, d), lambda i: (i, 0)), full(w["w_pq_t"]), full(w["keys"])],
        out_specs=[pick, pick, pick],
        out_shape=[jax.ShapeDtypeStruct((t, n_pick), I32), jax.ShapeDtypeStruct((t, n_pick), I32),
                   jax.ShapeDtypeStruct((t, n_pick), F32)],
        scratch_shapes=[pltpu.VMEM((n_half, N_KEYS, tr), BF16),
                        pltpu.VMEM((n_half, PEER_TOPK, tr), F32),
                        pltpu.VMEM((n_half, PEER_TOPK, tr), F32),
                        pltpu.VMEM((N_CAND_PAD, tr), F32),
                        pltpu.VMEM((N_CAND_PAD, tr), F32),
                        pltpu.VMEM((PEER_HEADS, PEER_TOPK, tr), F32),
                        pltpu.VMEM((PEER_HEADS, PEER_TOPK, tr), F32)],
        compiler_params=_params(32, 1),
        name="route",
    )(hn, w["w_pq_t"], w["keys"])


def _expert_kernel(hn_ref, x1_ref, i_ref, j_ref, g_ref, u_ref, v_ref, gfin_ref, y_ref,
                   w_ref, acc_ref, *, tt, ni, n_et):
    ne = pl.program_id(1)

    @pl.when(ne == 0)
    def _():
        acc_ref[...] = jnp.zeros_like(acc_ref)
        key_id = lax.broadcasted_iota(I32, (N_KEYS, N_KEYS), 0)
        zero = jnp.zeros((N_KEYS, N_KEYS), BF16)

        def one_hots(t):
            irow = i_ref[pl.ds(t, 1), :]
            jrow = j_ref[pl.ds(t, 1), :]
            grow = g_ref[pl.ds(t, 1), :]
            at = jnp.where(key_id == irow, grow, 0.0).astype(BF16)
            bt = jnp.where(key_id == jrow, 1.0, 0.0).astype(BF16)
            return at, bt

        def token_pair(tp, _):
            at0, bt0 = one_hots(tp)
            at1, bt1 = one_hots(tp + tt // 2)
            lhs = jnp.concatenate([at0, at1], axis=1)
            rhs = jnp.concatenate([jnp.concatenate([bt0, zero], axis=1),
                                   jnp.concatenate([zero, bt1], axis=1)], axis=0)
            w = lax.dot_general(lhs, rhs, (((1,), (1,)), ((), ())), preferred_element_type=F32)
            lo = lax.bitcast_convert_type(w[:, :N_KEYS].astype(BF16).astype(F32), U32)
            hi = lax.bitcast_convert_type(w[:, N_KEYS:].astype(BF16).astype(F32), U32)
            row0 = pl.multiple_of(tp * W_PITCH, SUBLANES)
            w_ref[pl.ds(row0, N_KEYS), :] = hi | lax.shift_right_logical(lo, jnp.uint32(16))
            return 0

        lax.fori_loop(0, tt // 2, token_pair, 0, unroll=32)

    a = lax.dot_general(hn_ref[...], u_ref[...], (((1,), (1,)), ((), ())),
                        preferred_element_type=F32)
    act = 0.5 * a * (1.0 + lax.erf(a * np.float32(np.sqrt(0.5))))

    def gate_columns(ii):
        words = w_ref[pl.ds(ne * ni + ii, tt // 2, stride=W_PITCH), :]
        first = lax.bitcast_convert_type(lax.shift_left(words, jnp.uint32(16)), F32)
        second = lax.bitcast_convert_type(words & jnp.uint32(0xFFFF0000), F32)
        return jnp.concatenate([first, second], axis=0)

    w = jnp.concatenate([gate_columns(ii) for ii in range(ni)], axis=1)
    acc_ref[...] += jnp.dot((w * act).astype(BF16), v_ref[...], preferred_element_type=F32)

    @pl.when(ne == n_et - 1)
    def _():
        y_ref[...] = _rms(x1_ref[...] + acc_ref[...], gfin_ref[...])


def _experts(hn, x1, pick_i, pick_j, pick_g, w, g_final, *, tt, ni):
    t, d = hn.shape
    n_exp = w["u"].shape[0]
    et = ni * N_KEYS
    n_et = n_exp // et
    n_pick = PEER_HEADS * PEER_TOPK
    tok = lambda width: pl.BlockSpec((tt, width), lambda i, e: (i, 0))
    return pl.pallas_call(
        functools.partial(_expert_kernel, tt=tt, ni=ni, n_et=n_et),
        grid=(t // tt, n_et),
        in_specs=[tok(d), tok(d), tok(n_pick), tok(n_pick), tok(n_pick),
                  pl.BlockSpec((et, d), lambda i, e: (e, 0)),
                  pl.BlockSpec((et, d), lambda i, e: (e, 0)),
                  pl.BlockSpec((1, d), lambda i, e: (0, 0))],
        out_specs=tok(d),
        out_shape=jax.ShapeDtypeStruct((t, d), F32),
        scratch_shapes=[pltpu.VMEM((tt // 2 * W_PITCH, N_KEYS), U32), pltpu.VMEM((tt, d), F32)],
        compiler_params=_params(56, 2),
        name="experts",
    )(hn, x1, pick_i, pick_j, pick_g, w["u"], w["v"], g_final)


def _rope_tables(pos, scale):
    freqs = ROPE_THETA ** (-jnp.arange(ROPE_HALF, dtype=F32) / ROPE_HALF)
    ang = pos.astype(F32)[:, None] * freqs[None, :]
    cos, sin = jnp.cos(ang), jnp.sin(ang)
    n = pos.shape[0]
    ones = jnp.ones((n, QK_NOPE), F32)
    tail = jnp.ones((n, HEAD_PAD - QK_NOPE - QK_ROPE), F32)
    cos_tab = jnp.concatenate([ones, cos, cos, tail], axis=1)
    sin_tab = jnp.concatenate([0 * ones, -sin, sin, 0 * tail], axis=1)
    return cos_tab * scale, sin_tab * scale, cos_tab, sin_tab


def _prep_weights(l, g_attn_norm, w_in, g_q, w_uq, g_kv, w_ukv, w_conv, g_mla_out, g_conv_out,
                  w_out, g_ffn_norm, w_peer_q, peer_keys, peer_u, peer_v):
    d = w_in.shape[1]
    zeros = lambda n: jnp.zeros((d, n), w_in.dtype)
    kpe_end = KV_RANK + Q_RANK + QK_ROPE
    w_in_r = jnp.concatenate(
        [w_in[l][:, :CKV0 + KV_RANK], zeros(ROPE_LO), w_in[l][:, CKV0 + KV_RANK:kpe_end],
         zeros(HEAD_PAD - ROPE_LO - QK_ROPE), w_in[l][:, kpe_end:]], axis=1)
    w_uq_r = jnp.pad(w_uq[l].reshape(Q_RANK, HEADS, QK_NOPE + QK_ROPE),
                     ((0, 0), (0, 0), (0, HEAD_PAD - QK_NOPE - QK_ROPE)))
    kv3 = w_ukv[l].reshape(KV_RANK, HEADS, QK_NOPE + V_DIM)
    w_k = jnp.pad(kv3[:, :, :QK_NOPE], ((0, 0), (0, 0), (0, HEAD_PAD - QK_NOPE)))
    w_vt = kv3[:, :, QK_NOPE:].reshape(KV_RANK, HEADS * V_DIM).T
    place = jnp.zeros((QK_ROPE, HEAD_PAD), F32).at[
        jnp.arange(QK_ROPE), ROPE_LO + jnp.arange(QK_ROPE)].set(1.0)
    row = lambda g: g[l].reshape(1, -1).astype(F32)
    mla_w = HEADS * V_DIM
    return {
        "w_in": w_in_r.astype(BF16), "g_attn": row(g_attn_norm), "g_q": row(g_q),
        "w_uq": w_uq_r.reshape(Q_RANK, HEADS * HEAD_PAD).astype(BF16), "g_kv": row(g_kv),
        "w_conv": w_conv[l].astype(F32), "g_conv": row(g_conv_out),
        "w_k": w_k.reshape(KV_RANK, HEADS * HEAD_PAD).astype(BF16), "w_vt": w_vt.astype(BF16),
        "place": place.astype(BF16),
        "g_mla": row(g_mla_out), "w_o_attn": w_out[l][:mla_w].astype(BF16),
        "w_o_conv": w_out[l][mla_w:].astype(BF16), "g_ffn": row(g_ffn_norm),
        "w_pq_t": w_peer_q[l].T.astype(BF16),
        "keys": peer_keys[l].reshape(2 * PEER_HEADS, N_KEYS, -1).astype(BF16),
        "u": peer_u[l].astype(BF16), "v": peer_v[l].astype(BF16),
    }


def _tile(n, pref):
    t = min(n, pref)
    assert n % t == 0, (n, pref)
    return t


def _layer(x, pos0, s_valid, past_kv, past_kpe, past_conv, w, g_final):
    b, s, d = x.shape
    scale = float(QK_NOPE + QK_ROPE) ** -0.5 * float(np.log2(np.e))
    tabs = _rope_tables(pos0 + jnp.arange(s, dtype=jnp.int32), scale)
    if past_conv is None:
        past_conv = jnp.zeros((b, 2, CONV_WIDTH), F32)
    q, ckv, kpe, convn, new_conv = _front(x, past_conv, tabs, w, ts=_tile(s, 512), s_valid=s_valid)

    if past_kv is None:
        ckv_all, kpe_all, sk_valid = ckv, kpe, s
    else:
        ckv_all = jnp.concatenate([past_kv, ckv[:, :s_valid]], axis=1)
        kpe_all = jnp.concatenate([past_kpe, kpe[:, :s_valid]], axis=1)
        sk_valid = ckv_all.shape[1]
    tk = 512 if sk_valid % 512 == 0 else 256
    pad = -sk_valid % tk
    if pad:
        ckv_all = jnp.pad(ckv_all, ((0, 0), (0, pad), (0, 0)))
        kpe_all = jnp.pad(kpe_all, ((0, 0), (0, pad), (0, 0)))
    k, vt = _kv_up(ckv_all, kpe_all, w, tk=tk)
    attn_t = _attention(q, k, vt, tq=_tile(s, 512), q_off=pos0, sk_valid=sk_valid)
    x1, hn = _merge(x, attn_t, convn, w, ts=_tile(s, 512))

    t = b * s
    hn2 = hn.reshape(t, d)
    pick_i, pick_j, pick_g = _route(hn2, w, tr=_tile(t, 512))
    y = _experts(hn2, x1.reshape(t, d), pick_i, pick_j, pick_g, w, g_final, tt=_tile(t, 512), ni=16)
    return y.reshape(b, s, d), ckv, kpe, new_conv


def kernel(x_prompt, x_sample, cache_kv_latent, cache_k_rope, state_conv, g_attn_norm, w_in, g_q, w_uq, g_kv, w_ukv, w_conv, g_mla_out, g_conv_out, w_out, g_ffn_norm, w_peer_q, peer_keys, peer_u, peer_v, g_final):
    depth = w_in.shape[0]
    assert depth == 1, "the final norm is fused into the last layer; one layer supported"
    past_len = cache_kv_latent.shape[2]
    s_dec = x_sample.shape[1]
    s_pad = -(-s_dec // LANES) * LANES
    hs = jnp.pad(x_sample, ((0, 0), (0, s_pad - s_dec), (0, 0)))
    gfin = g_final.reshape(1, -1).astype(F32)
    w = _prep_weights(0, g_attn_norm, w_in, g_q, w_uq, g_kv, w_ukv, w_conv, g_mla_out, g_conv_out,
                      w_out, g_ffn_norm, w_peer_q, peer_keys, peer_u, peer_v)
    yp, kv_p, kpe_p, conv_p = _layer(x_prompt, 0, x_prompt.shape[1], None, None, None, w, gfin)
    ys, kv_s, kpe_s, conv_s = _layer(hs, past_len, s_dec, cache_kv_latent[0], cache_k_rope[0],
                                     state_conv[0], w, gfin)
    return (yp, ys[:, :s_dec],
            kv_p[None], kpe_p[None], conv_p[None],
            kv_s[None, :, :s_dec], kpe_s[None, :, :s_dec], conv_s[None])
```

```python
import functools

import numpy as np
import jax
import jax.numpy as jnp
from jax import lax
from jax.experimental import pallas as pl
from jax.experimental.pallas import tpu as pltpu

F32 = jnp.float32
BF16 = jnp.bfloat16
I32 = jnp.int32
U32 = jnp.uint32

EPS = 1e-6
NEG_INF = -1e30
CHUNK_SHIFT = 6
HEADS = 8
QK_NOPE = 64
QK_ROPE = 32
V_DIM = 64
VT_ROWS = 80
Q_RANK = 384
KV_RANK = 256
CONV_WIDTH = 512
ROPE_THETA = 10000.0
HEAD_PAD = 128
ROPE_LO = QK_NOPE
ROPE_HALF = QK_ROPE // 2
PEER_HEADS = 8
N_KEYS = 128
PEER_TOPK = 16
LANES = 128
SUBLANES = 8
W_PITCH = 136

CQ0, CKV0, KPE0, UC0, BG0, CG0, ZW = 0, 384, 640, 768, 1280, 1792, 2304

CANDS = sorted(
    [(a, b) for a in range(PEER_TOPK) for b in range(PEER_TOPK) if (a + 1) * (b + 1) <= PEER_TOPK],
    key=lambda ab: ab[0] * PEER_TOPK + ab[1])
N_CAND = len(CANDS)
N_CAND_PAD = -(-N_CAND // SUBLANES) * SUBLANES


def _rms(x, g):
    ms = jnp.mean(x * x, axis=-1, keepdims=True)
    return x * lax.rsqrt(ms + EPS) * g


def _swap_rope_halves(v):
    lane = lax.broadcasted_iota(I32, v.shape, 1)
    return jnp.where(lane < ROPE_LO + ROPE_HALF,
                     pltpu.roll(v, LANES - ROPE_HALF, 1), pltpu.roll(v, ROPE_HALF, 1))


def _params(vmem_mb, n_axes):
    return pltpu.CompilerParams(dimension_semantics=("arbitrary",) * n_axes,
                                vmem_limit_bytes=vmem_mb * 1024 * 1024)


def _front_kernel(x_ref, win_ref, gattn_ref, gq_ref, wuq_ref, gkv_ref, wconv_ref, gconv_ref,
                  cq_ref, sq_ref, ck_ref, sk_ref, past_ref,
                  q_ref, ckv_ref, kpe_ref, convn_ref, newconv_ref, carry_ref,
                  *, ts, nc_tile, nc_row):
    si = pl.program_id(1)
    x = x_ref[0]
    h = _rms(x, gattn_ref[...]).astype(BF16)
    z = jnp.dot(h, win_ref[...], preferred_element_type=F32)

    cqn = _rms(z[:, CQ0:CKV0], gq_ref[...]).astype(BF16)
    q = jnp.dot(cqn, wuq_ref[...], preferred_element_type=F32)
    cq = cq_ref[...]
    sq = sq_ref[...]
    for hd in range(HEADS):
        qh = q[:, hd * HEAD_PAD:(hd + 1) * HEAD_PAD]
        q_ref[0, hd] = (qh * cq + _swap_rope_halves(qh) * sq).astype(BF16)

    ckv_ref[0] = _rms(z[:, CKV0:KPE0], gkv_ref[...])

    kp = z[:, KPE0:UC0]
    kp = kp * ck_ref[...] + _swap_rope_halves(kp) * sk_ref[...]
    kpe_ref[0] = kp[:, ROPE_LO:ROPE_LO + QK_ROPE]

    u = z[:, CG0:ZW] * z[:, UC0:BG0]

    @pl.when(si == 0)
    def _():
        carry_ref[0:2, :] = past_ref[0]

    prev2 = carry_ref[0:1, :]
    prev1 = carry_ref[1:2, :]
    row = lax.broadcasted_iota(I32, u.shape, 0)
    u1 = jnp.where(row == 0, prev1, pltpu.roll(u, 1, 0))
    u2 = jnp.where(row == 0, prev2, jnp.where(row == 1, prev1, pltpu.roll(u, 2, 0)))
    conv = u2 * wconv_ref[0:1, :] + u1 * wconv_ref[1:2, :] + u * wconv_ref[2:3, :]
    convn_ref[0] = _rms(z[:, BG0:CG0] * conv, gconv_ref[...]).astype(BF16)
    carry_ref[0:2, :] = u[ts - 2:ts, :]

    @pl.when(si == nc_tile)
    def _():
        newconv_ref[0] = u[nc_row:nc_row + 2, :]


def _front(x, past_conv, tabs, w, *, ts, s_valid):
    b, s, d = x.shape
    cq, sq, ck, sk = tabs
    full = lambda a: pl.BlockSpec(a.shape, lambda i, j: (0,) * a.ndim)
    tab = pl.BlockSpec((ts, LANES), lambda i, j: (j, 0))
    kern = functools.partial(_front_kernel, ts=ts, nc_tile=(s_valid - 2) // ts,
                             nc_row=(s_valid - 2) % ts)
    return pl.pallas_call(
        kern,
        grid=(b, s // ts),
        in_specs=[pl.BlockSpec((1, ts, d), lambda i, j: (i, j, 0)),
                  full(w["w_in"]), full(w["g_attn"]), full(w["g_q"]), full(w["w_uq"]),
                  full(w["g_kv"]), full(w["w_conv"]), full(w["g_conv"]),
                  tab, tab, tab, tab,
                  pl.BlockSpec((1, 2, CONV_WIDTH), lambda i, j: (i, 0, 0))],
        out_specs=[pl.BlockSpec((1, HEADS, ts, HEAD_PAD), lambda i, j: (i, 0, j, 0)),
                   pl.BlockSpec((1, ts, KV_RANK), lambda i, j: (i, j, 0)),
                   pl.BlockSpec((1, ts, QK_ROPE), lambda i, j: (i, j, 0)),
                   pl.BlockSpec((1, ts, CONV_WIDTH), lambda i, j: (i, j, 0)),
                   pl.BlockSpec((1, 2, CONV_WIDTH), lambda i, j: (i, 0, 0))],
        out_shape=[jax.ShapeDtypeStruct((b, HEADS, s, HEAD_PAD), BF16),
                   jax.ShapeDtypeStruct((b, s, KV_RANK), F32),
                   jax.ShapeDtypeStruct((b, s, QK_ROPE), F32),
                   jax.ShapeDtypeStruct((b, s, CONV_WIDTH), BF16),
                   jax.ShapeDtypeStruct((b, 2, CONV_WIDTH), F32)],
        scratch_shapes=[pltpu.VMEM((SUBLANES, CONV_WIDTH), F32)],
        compiler_params=_params(48, 2),
        name="front",
    )(x, w["w_in"], w["g_attn"], w["g_q"], w["w_uq"], w["g_kv"], w["w_conv"], w["g_conv"],
      cq, sq, ck, sk, past_conv)


def _kvup_kernel(ckv_ref, kpe_ref, wk_ref, wvt_ref, place_ref, k_ref, vt_ref):
    c = ckv_ref[0].astype(BF16)
    kn = jnp.dot(c, wk_ref[...], preferred_element_type=F32)
    kp = jnp.dot(kpe_ref[0].astype(BF16), place_ref[...], preferred_element_type=F32)
    for hd in range(HEADS):
        k_ref[0, hd] = (kn[:, hd * HEAD_PAD:(hd + 1) * HEAD_PAD] + kp).astype(BF16)
    vt = lax.dot_general(wvt_ref[...], c, (((1,), (1,)), ((), ())),
                         preferred_element_type=F32)
    ones_row = jnp.where(lax.broadcasted_iota(I32, (VT_ROWS - V_DIM, vt.shape[1]), 0) == 0,
                         1.0, 0.0).astype(BF16)
    for hd in range(HEADS):
        vt_ref[0, hd, 0, 0:V_DIM, :] = vt[hd * V_DIM:(hd + 1) * V_DIM, :].astype(BF16)
        vt_ref[0, hd, 0, V_DIM:VT_ROWS, :] = ones_row


def _kv_up(ckv, kpe, w, *, tk):
    b, sk, _ = ckv.shape
    nkb = sk // tk
    full = lambda a: pl.BlockSpec(a.shape, lambda i, j: (0,) * a.ndim)
    return pl.pallas_call(
        _kvup_kernel,
        grid=(b, nkb),
        in_specs=[pl.BlockSpec((1, tk, KV_RANK), lambda i, j: (i, j, 0)),
                  pl.BlockSpec((1, tk, QK_ROPE), lambda i, j: (i, j, 0)),
                  full(w["w_k"]), full(w["w_vt"]), full(w["place"])],
        out_specs=[pl.BlockSpec((1, HEADS, tk, HEAD_PAD), lambda i, j: (i, 0, j, 0)),
                   pl.BlockSpec((1, HEADS, 1, VT_ROWS, tk), lambda i, j: (i, 0, j, 0, 0))],
        out_shape=[jax.ShapeDtypeStruct((b, HEADS, sk, HEAD_PAD), BF16),
                   jax.ShapeDtypeStruct((b, HEADS, nkb, VT_ROWS, tk), BF16)],
        compiler_params=_params(32, 2),
        name="kv_up",
    )(ckv, kpe, w["w_k"], w["w_vt"], w["place"])


def _attn_kernel(q_ref, k_ref, vt_ref, o_ref, sa_ref, sb_ref,
                 *, tq, tk, nq, nkb, q_off, sk_valid):
    def q_block(qi, _):
        q = q_ref[0, 0, pl.ds(pl.multiple_of(qi * tq, tq), tq), :]
        q0 = q_off + qi * tq
        qch = lax.shift_right_logical(q0 + lax.broadcasted_iota(I32, (1, tq), 1), CHUNK_SHIFT)
        last_visible = lax.shift_left(lax.shift_right_logical(q0 + tq - 1, CHUNK_SHIFT) + 1,
                                      CHUNK_SHIFT) - 1
        nb = jnp.minimum(nkb, last_visible // tk + 1)
        first_chunk_end = lax.shift_left(lax.shift_right_logical(q0, CHUNK_SHIFT) + 1,
                                         CHUNK_SHIFT)
        n_full = jnp.minimum(jnp.minimum(first_chunk_end // tk, sk_valid // tk), nb)

        def scores(kb):
            k = k_ref[0, 0, pl.ds(pl.multiple_of(kb * tk, tk), tk), :]
            return lax.dot_general(k, q, (((1,), (1,)), ((), ())), preferred_element_type=F32)

        def update(s_ref, kb, carry):
            m, acc = carry
            m_new = jnp.maximum(m, jnp.max(s_ref[...], axis=0, keepdims=True))
            p = jnp.exp2(s_ref[...] - m_new).astype(BF16)
            acc = jnp.exp2(m - m_new) * acc + jnp.dot(vt_ref[0, 0, kb], p,
                                                      preferred_element_type=F32)
            return m_new, acc

        def pair(j, carry):
            sb_ref[...] = scores(2 * j + 1)
            carry = update(sa_ref, 2 * j, carry)
            sa_ref[...] = scores(jnp.minimum(2 * j + 2, nkb - 1))
            return update(sb_ref, 2 * j + 1, carry)

        def masked_update(s_ref, kb, carry):
            kpos = kb * tk + lax.broadcasted_iota(I32, (tk, 1), 0)
            vis = (lax.shift_right_logical(kpos, CHUNK_SHIFT) <= qch) & (kpos < sk_valid)
            s_ref[...] = jnp.where(vis, s_ref[...], NEG_INF)
            return update(s_ref, kb, carry)

        n_pairs = n_full // 2
        base = 2 * n_pairs
        rest = nb - base

        def tail_one(_, carry):
            return masked_update(sa_ref, base, carry)

        def tail_two(_, carry):
            sb_ref[...] = scores(base + 1)
            carry = masked_update(sa_ref, base, carry)
            return masked_update(sb_ref, base + 1, carry)

        def tail_more(kb, carry):
            sa_ref[...] = scores(kb)
            return masked_update(sa_ref, kb, carry)

        sa_ref[...] = scores(0)
        carry = (jnp.full((1, tq), NEG_INF, F32), jnp.zeros((VT_ROWS, tq), F32))
        carry = lax.fori_loop(0, n_pairs, pair, carry)
        carry = lax.fori_loop(0, (rest == 1).astype(I32), tail_one, carry)
        carry = lax.fori_loop(0, (rest >= 2).astype(I32), tail_two, carry)
        _, acc = lax.fori_loop(base + 2, nb, tail_more, carry)
        o_ref[0, 0, qi] = acc[:V_DIM] / acc[V_DIM:V_DIM + 1]
        return 0

    lax.fori_loop(0, nq, q_block, 0)


def _attention(q, k, vt, *, tq, q_off, sk_valid):
    b, _, s, _ = q.shape
    _, _, nkb, _, tk = vt.shape
    sk = k.shape[2]
    nq = s // tq
    kern = functools.partial(_attn_kernel, tq=tq, tk=tk, nq=nq, nkb=nkb, q_off=q_off,
                             sk_valid=sk_valid)
    return pl.pallas_call(
        kern,
        grid=(b, HEADS),
        in_specs=[pl.BlockSpec((1, 1, s, HEAD_PAD), lambda i, h: (i, h, 0, 0)),
                  pl.BlockSpec((1, 1, sk, HEAD_PAD), lambda i, h: (i, h, 0, 0)),
                  pl.BlockSpec((1, 1, nkb, VT_ROWS, tk), lambda i, h: (i, h, 0, 0, 0))],
        out_specs=pl.BlockSpec((1, 1, nq, V_DIM, tq), lambda i, h: (i, h, 0, 0, 0)),
        out_shape=jax.ShapeDtypeStruct((b, HEADS, nq, V_DIM, tq), F32),
        scratch_shapes=[pltpu.VMEM((tk, tq), F32), pltpu.VMEM((tk, tq), F32)],
        compiler_params=_params(32, 2),
        name="attn",
    )(q, k, vt)


def _merge_kernel(x_ref, at_ref, cn_ref, gm_ref, woa_ref, woc_ref, gf_ref, x1_ref, hn_ref, *, ts):
    a = at_ref[0, :, 0].reshape(HEADS * V_DIM, ts).T
    an = _rms(a, gm_ref[...]).astype(BF16)
    mix = (jnp.dot(an, woa_ref[...], preferred_element_type=F32)
           + jnp.dot(cn_ref[0], woc_ref[...], preferred_element_type=F32))
    x1 = x_ref[0] + mix
    x1_ref[0] = x1
    hn_ref[0] = _rms(x1, gf_ref[...]).astype(BF16)


def _merge(x, attn_t, convn, w, *, ts):
    b, s, d = x.shape
    assert attn_t.shape[4] == ts, "attention query blocks and merge row tiles must coincide"
    full = lambda a: pl.BlockSpec(a.shape, lambda i, j: (0,) * a.ndim)
    return pl.pallas_call(
        functools.partial(_merge_kernel, ts=ts),
        grid=(b, s // ts),
        in_specs=[pl.BlockSpec((1, ts, d), lambda i, j: (i, j, 0)),
                  pl.BlockSpec((1, HEADS, 1, V_DIM, ts), lambda i, j: (i, 0, j, 0, 0)),
                  pl.BlockSpec((1, ts, CONV_WIDTH), lambda i, j: (i, j, 0)),
                  full(w["g_mla"]), full(w["w_o_attn"]), full(w["w_o_conv"]), full(w["g_ffn"])],
        out_specs=[pl.BlockSpec((1, ts, d), lambda i, j: (i, j, 0)),
                   pl.BlockSpec((1, ts, d), lambda i, j: (i, j, 0))],
        out_shape=[jax.ShapeDtypeStruct((b, s, d), F32), jax.ShapeDtypeStruct((b, s, d), BF16)],
        compiler_params=_params(32, 2),
        name="merge",
    )(x, attn_t, convn, w["g_mla"], w["w_o_attn"], w["w_o_conv"], w["g_ffn"])


def _route_kernel(hn_ref, wq_ref, keys_ref, i_out, j_out, g_out,
                  qt_ref, sv_ref, si_ref, cand_ref, ce_ref, eb_ref, gb_ref, *, tr):
    n_half = 2 * PEER_HEADS
    qt = lax.dot_general(wq_ref[...], hn_ref[...], (((1,), (1,)), ((), ())),
                         preferred_element_type=F32)
    qt_ref[...] = qt.astype(BF16).reshape(n_half, N_KEYS, tr)

    def half_body(hc, _):
        s = jnp.dot(keys_ref[hc], qt_ref[hc], preferred_element_type=F32)
        key_id = lax.broadcasted_iota(I32, (N_KEYS, tr), 0).astype(F32)
        for k in range(PEER_TOPK):
            m = jnp.max(s, axis=0, keepdims=True)
            idx = jnp.min(jnp.where(s == m, key_id, float(N_KEYS)), axis=0, keepdims=True)
            sv_ref[hc, k:k + 1, :] = m
            si_ref[hc, k:k + 1, :] = idx
            s = jnp.where(key_id == idx, -jnp.inf, s)
        return 0

    lax.fori_loop(0, n_half, half_body, 0, unroll=2)

    cand_ref[N_CAND:N_CAND_PAD, :] = jnp.full((N_CAND_PAD - N_CAND, tr), -jnp.inf, F32)
    ce_ref[N_CAND:N_CAND_PAD, :] = jnp.zeros((N_CAND_PAD - N_CAND, tr), F32)

    def head_body(hd, _):
        for c, (a, b) in enumerate(CANDS):
            cand_ref[c:c + 1, :] = sv_ref[2 * hd, a:a + 1, :] + sv_ref[2 * hd + 1, b:b + 1, :]
            ce_ref[c:c + 1, :] = (si_ref[2 * hd, a:a + 1, :] * float(N_KEYS)
                                  + si_ref[2 * hd + 1, b:b + 1, :])
        cand = cand_ref[...]
        ce = ce_ref[...]
        slot = lax.broadcasted_iota(I32, (N_CAND_PAD, tr), 0).astype(F32)
        for k in range(PEER_TOPK):
            m = jnp.max(cand, axis=0, keepdims=True)
            idx = jnp.min(jnp.where(cand == m, slot, float(N_CAND_PAD)), axis=0, keepdims=True)
            sel = slot == idx
            gb_ref[hd, k:k + 1, :] = m
            eb_ref[hd, k:k + 1, :] = jnp.max(jnp.where(sel, ce, -1.0), axis=0, keepdims=True)
            cand = jnp.where(sel, -jnp.inf, cand)
        best = gb_ref[hd]
        e = jnp.exp(best - best[0:1, :])
        gb_ref[hd] = e / jnp.sum(e, axis=0, keepdims=True)
        return 0

    lax.fori_loop(0, PEER_HEADS, head_body, 0, unroll=2)

    n_pick = PEER_HEADS * PEER_TOPK
    experts = eb_ref[...].reshape(n_pick, tr).T.astype(I32)
    i_out[...] = lax.shift_right_logical(experts, 7)
    j_out[...] = experts & (N_KEYS - 1)
    g_out[...] = gb_ref[...].reshape(n_pick, tr).T


def _route(hn, w, *, tr):
    t, d = hn.shape
    n_half = 2 * PEER_HEADS
    n_pick = PEER_HEADS * PEER_TOPK
    full = lambda a: pl.BlockSpec(a.shape, lambda i: (0,) * a.ndim)
    pick = pl.BlockSpec((tr, n_pick), lambda i: (i, 0))
    return pl.pallas_call(
        functools.partial(_route_kernel, tr=tr),
        grid=(t // tr,),
        in_specs=[pl.BlockSpec((tr, d), lambda i: (i, 0)), full(w["w_pq_t"]), full(w["keys"])],
        out_specs=[pick, pick, pick],
        out_shape=[jax.ShapeDtypeStruct((t, n_pick), I32), jax.ShapeDtypeStruct((t, n_pick), I32),
                   jax.ShapeDtypeStruct((t, n_pick), F32)],
        scratch_shapes=[pltpu.VMEM((n_half, N_KEYS, tr), BF16),
                        pltpu.VMEM((n_half, PEER_TOPK, tr), F32),
                        pltpu.VMEM((n_half, PEER_TOPK, tr), F32),
                        pltpu.VMEM((N_CAND_PAD, tr), F32),
                        pltpu.VMEM((N_CAND_PAD, tr), F32),
                        pltpu.VMEM((PEER_HEADS, PEER_TOPK, tr), F32),
                        pltpu.VMEM((PEER_HEADS, PEER_TOPK, tr), F32)],
        compiler_params=_params(32, 1),
        name="route",
    )(hn, w["w_pq_t"], w["keys"])


def _expert_kernel(hn_ref, x1_ref, i_ref, j_ref, g_ref, u_ref, v_ref, gfin_ref, y_ref,
                   w_ref, acc_ref, *, tt, ni, n_et):
    ne = pl.program_id(1)

    @pl.when(ne == 0)
    def _():
        acc_ref[...] = jnp.zeros_like(acc_ref)
        key_id = lax.broadcasted_iota(I32, (N_KEYS, N_KEYS), 0)
        zero = jnp.zeros((N_KEYS, N_KEYS), BF16)

        def one_hots(t):
            irow = i_ref[pl.ds(t, 1), :]
            jrow = j_ref[pl.ds(t, 1), :]
            grow = g_ref[pl.ds(t, 1), :]
            at = jnp.where(key_id == irow, grow, 0.0).astype(BF16)
            bt = jnp.where(key_id == jrow, 1.0, 0.0).astype(BF16)
            return at, bt

        def token_pair(tp, _):
            at0, bt0 = one_hots(tp)
            at1, bt1 = one_hots(tp + tt // 2)
            lhs = jnp.concatenate([at0, at1], axis=1)
            rhs = jnp.concatenate([jnp.concatenate([bt0, zero], axis=1),
                                   jnp.concatenate([zero, bt1], axis=1)], axis=0)
            w = lax.dot_general(lhs, rhs, (((1,), (1,)), ((), ())), preferred_element_type=F32)
            lo = lax.bitcast_convert_type(w[:, :N_KEYS].astype(BF16).astype(F32), U32)
            hi = lax.bitcast_convert_type(w[:, N_KEYS:].astype(BF16).astype(F32), U32)
            row0 = pl.multiple_of(tp * W_PITCH, SUBLANES)
            w_ref[pl.ds(row0, N_KEYS), :] = hi | lax.shift_right_logical(lo, jnp.uint32(16))
            return 0

        lax.fori_loop(0, tt // 2, token_pair, 0, unroll=32)

    a = lax.dot_general(hn_ref[...], u_ref[...], (((1,), (1,)), ((), ())),
                        preferred_element_type=F32)
    act = 0.5 * a * (1.0 + lax.erf(a * np.float32(np.sqrt(0.5))))

    def gate_columns(ii):
        words = w_ref[pl.ds(ne * ni + ii, tt // 2, stride=W_PITCH), :]
        first = lax.bitcast_convert_type(lax.shift_left(words, jnp.uint32(16)), F32)
        second = lax.bitcast_convert_type(words & jnp.uint32(0xFFFF0000), F32)
        return jnp.concatenate([first, second], axis=0)

    w = jnp.concatenate([gate_columns(ii) for ii in range(ni)], axis=1)
    acc_ref[...] += jnp.dot((w * act).astype(BF16), v_ref[...], preferred_element_type=F32)

    @pl.when(ne == n_et - 1)
    def _():
        y_ref[...] = _rms(x1_ref[...] + acc_ref[...], gfin_ref[...])


def _experts(hn, x1, pick_i, pick_j, pick_g, w, g_final, *, tt, ni):
    t, d = hn.shape
    n_exp = w["u"].shape[0]
    et = ni * N_KEYS
    n_et = n_exp // et
    n_pick = PEER_HEADS * PEER_TOPK
    tok = lambda width: pl.BlockSpec((tt, width), lambda i, e: (i, 0))
    return pl.pallas_call(
        functools.partial(_expert_kernel, tt=tt, ni=ni, n_et=n_et),
        grid=(t // tt, n_et),
        in_specs=[tok(d), tok(d), tok(n_pick), tok(n_pick), tok(n_pick),
                  pl.BlockSpec((et, d), lambda i, e: (e, 0)),
                  pl.BlockSpec((et, d), lambda i, e: (e, 0)),
                  pl.BlockSpec((1, d), lambda i, e: (0, 0))],
        out_specs=tok(d),
        out_shape=jax.ShapeDtypeStruct((t, d), F32),
        scratch_shapes=[pltpu.VMEM((tt // 2 * W_PITCH, N_KEYS), U32), pltpu.VMEM((tt, d), F32)],
        compiler_params=_params(56, 2),
        name="experts",
    )(hn, x1, pick_i, pick_j, pick_g, w["u"], w["v"], g_final)


def _rope_tables(pos, scale):
    freqs = ROPE_THETA ** (-jnp.arange(ROPE_HALF, dtype=F32) / ROPE_HALF)
    ang = pos.astype(F32)[:, None] * freqs[None, :]
    cos, sin = jnp.cos(ang), jnp.sin(ang)
    n = pos.shape[0]
    ones = jnp.ones((n, QK_NOPE), F32)
    tail = jnp.ones((n, HEAD_PAD - QK_NOPE - QK_ROPE), F32)
    cos_tab = jnp.concatenate([ones, cos, cos, tail], axis=1)
    sin_tab = jnp.concatenate([0 * ones, -sin, sin, 0 * tail], axis=1)
    return cos_tab * scale, sin_tab * scale, cos_tab, sin_tab


def _prep_weights(l, g_attn_norm, w_in, g_q, w_uq, g_kv, w_ukv, w_conv, g_mla_out, g_conv_out,
                  w_out, g_ffn_norm, w_peer_q, peer_keys, peer_u, peer_v):
    d = w_in.shape[1]
    zeros = lambda n: jnp.zeros((d, n), w_in.dtype)
    kpe_end = KV_RANK + Q_RANK + QK_ROPE
    w_in_r = jnp.concatenate(
        [w_in[l][:, :CKV0 + KV_RANK], zeros(ROPE_LO), w_in[l][:, CKV0 + KV_RANK:kpe_end],
         zeros(HEAD_PAD - ROPE_LO - QK_ROPE), w_in[l][:, kpe_end:]], axis=1)
    w_uq_r = jnp.pad(w_uq[l].reshape(Q_RANK, HEADS, QK_NOPE + QK_ROPE),
                     ((0, 0), (0, 0), (0, HEAD_PAD - QK_NOPE - QK_ROPE)))
    kv3 = w_ukv[l].reshape(KV_RANK, HEADS, QK_NOPE + V_DIM)
    w_k = jnp.pad(kv3[:, :, :QK_NOPE], ((0, 0), (0, 0), (0, HEAD_PAD - QK_NOPE)))
    w_vt = kv3[:, :, QK_NOPE:].reshape(KV_RANK, HEADS * V_DIM).T
    place = jnp.zeros((QK_ROPE, HEAD_PAD), F32).at[
        jnp.arange(QK_ROPE), ROPE_LO + jnp.arange(QK_ROPE)].set(1.0)
    row = lambda g: g[l].reshape(1, -1).astype(F32)
    mla_w = HEADS * V_DIM
    return {
        "w_in": w_in_r.astype(BF16), "g_attn": row(g_attn_norm), "g_q": row(g_q),
        "w_uq": w_uq_r.reshape(Q_RANK, HEADS * HEAD_PAD).astype(BF16), "g_kv": row(g_kv),
        "w_conv": w_conv[l].astype(F32), "g_conv": row(g_conv_out),
        "w_k": w_k.reshape(KV_RANK, HEADS * HEAD_PAD).astype(BF16), "w_vt": w_vt.astype(BF16),
        "place": place.astype(BF16),
        "g_mla": row(g_mla_out), "w_o_attn": w_out[l][:mla_w].astype(BF16),
        "w_o_conv": w_out[l][mla_w:].astype(BF16), "g_ffn": row(g_ffn_norm),
        "w_pq_t": w_peer_q[l].T.astype(BF16),
        "keys": peer_keys[l].reshape(2 * PEER_HEADS, N_KEYS, -1).astype(BF16),
        "u": peer_u[l].astype(BF16), "v": peer_v[l].astype(BF16),
    }


def _tile(n, pref):
    t = min(n, pref)
    assert n % t == 0, (n, pref)
    return t


def _layer(x, pos0, s_valid, past_kv, past_kpe, past_conv, w, g_final):
    b, s, d = x.shape
    scale = float(QK_NOPE + QK_ROPE) ** -0.5 * float(np.log2(np.e))
    tabs = _rope_tables(pos0 + jnp.arange(s, dtype=jnp.int32), scale)
    if past_conv is None:
        past_conv = jnp.zeros((b, 2, CONV_WIDTH), F32)
    q, ckv, kpe, convn, new_conv = _front(x, past_conv, tabs, w, ts=_tile(s, 512), s_valid=s_valid)

    if past_kv is None:
        ckv_all, kpe_all, sk_valid = ckv, kpe, s
    else:
        ckv_all = jnp.concatenate([past_kv, ckv[:, :s_valid]], axis=1)
        kpe_all = jnp.concatenate([past_kpe, kpe[:, :s_valid]], axis=1)
        sk_valid = ckv_all.shape[1]
    tk = 512 if sk_valid % 512 == 0 else 256
    pad = -sk_valid % tk
    if pad:
        ckv_all = jnp.pad(ckv_all, ((0, 0), (0, pad), (0, 0)))
        kpe_all = jnp.pad(kpe_all, ((0, 0), (0, pad), (0, 0)))
    k, vt = _kv_up(ckv_all, kpe_all, w, tk=tk)
    attn_t = _attention(q, k, vt, tq=_tile(s, 512), q_off=pos0, sk_valid=sk_valid)
    x1, hn = _merge(x, attn_t, convn, w, ts=_tile(s, 512))

    t = b * s_valid
    hn2 = hn[:, :s_valid].reshape(t, d)
    x12 = x1[:, :s_valid].reshape(t, d)
    pick_i, pick_j, pick_g = _route(hn2, w, tr=_tile(t, 512))
    y = _experts(hn2, x12, pick_i, pick_j, pick_g, w, g_final, tt=_tile(t, 512), ni=16)
    return y.reshape(b, s_valid, d), ckv, kpe, new_conv


def kernel(x_prompt, x_sample, cache_kv_latent, cache_k_rope, state_conv, g_attn_norm, w_in, g_q, w_uq, g_kv, w_ukv, w_conv, g_mla_out, g_conv_out, w_out, g_ffn_norm, w_peer_q, peer_keys, peer_u, peer_v, g_final):
    depth = w_in.shape[0]
    assert depth == 1, "the final norm is fused into the last layer; one layer supported"
    past_len = cache_kv_latent.shape[2]
    s_dec = x_sample.shape[1]
    s_pad = -(-s_dec // LANES) * LANES
    hs = jnp.pad(x_sample, ((0, 0), (0, s_pad - s_dec), (0, 0)))
    gfin = g_final.reshape(1, -1).astype(F32)
    w = _prep_weights(0, g_attn_norm, w_in, g_q, w_uq, g_kv, w_ukv, w_conv, g_mla_out, g_conv_out,
                      w_out, g_ffn_norm, w_peer_q, peer_keys, peer_u, peer_v)
    yp, kv_p, kpe_p, conv_p = _layer(x_prompt, 0, x_prompt.shape[1], None, None, None, w, gfin)
    ys, kv_s, kpe_s, conv_s = _layer(hs, past_len, s_dec, cache_kv_latent[0], cache_k_rope[0],
                                     state_conv[0], w, gfin)
    return (yp, ys[:, :s_dec],
            kv_p[None], kpe_p[None], conv_p[None],
            kv_s[None, :, :s_dec], kpe_s[None, :, :s_dec], conv_s[None])
```

```python
import functools

import numpy as np
import jax
import jax.numpy as jnp
from jax import lax
from jax.experimental import pallas as pl
from jax.experimental.pallas import tpu as pltpu

F32 = jnp.float32
BF16 = jnp.bfloat16
I32 = jnp.int32
U32 = jnp.uint32

EPS = 1e-6
NEG_INF = -1e30
CHUNK_SHIFT = 6
HEADS = 8
QK_NOPE = 64
QK_ROPE = 32
V_DIM = 64
VT_ROWS = 80
Q_RANK = 384
KV_RANK = 256
CONV_WIDTH = 512
ROPE_THETA = 10000.0
HEAD_PAD = 128
ROPE_LO = QK_NOPE
ROPE_HALF = QK_ROPE // 2
PEER_HEADS = 8
N_KEYS = 128
PEER_TOPK = 16
LANES = 128
SUBLANES = 8
W_PITCH = 136

CQ0, CKV0, KPE0, UC0, BG0, CG0, ZW = 0, 384, 640, 768, 1280, 1792, 2304

CANDS = sorted(
    [(a, b) for a in range(PEER_TOPK) for b in range(PEER_TOPK) if (a + 1) * (b + 1) <= PEER_TOPK],
    key=lambda ab: ab[0] * PEER_TOPK + ab[1])
N_CAND = len(CANDS)
N_CAND_PAD = -(-N_CAND // SUBLANES) * SUBLANES


def _rms(x, g):
    ms = jnp.mean(x * x, axis=-1, keepdims=True)
    return x * lax.rsqrt(ms + EPS) * g


def _swap_rope_halves(v):
    lane = lax.broadcasted_iota(I32, v.shape, 1)
    return jnp.where(lane < ROPE_LO + ROPE_HALF,
                     pltpu.roll(v, LANES - ROPE_HALF, 1), pltpu.roll(v, ROPE_HALF, 1))


def _params(vmem_mb, n_axes):
    return pltpu.CompilerParams(dimension_semantics=("arbitrary",) * n_axes,
                                vmem_limit_bytes=vmem_mb * 1024 * 1024)


def _front_kernel(x_ref, win_ref, gattn_ref, gq_ref, wuq_ref, gkv_ref, wconv_ref, gconv_ref,
                  cq_ref, sq_ref, ck_ref, sk_ref, past_ref,
                  q_ref, ckv_ref, kpe_ref, convn_ref, newconv_ref, carry_ref,
                  *, ts, nc_tile, nc_row):
    si = pl.program_id(1)
    x = x_ref[0]
    h = _rms(x, gattn_ref[...]).astype(BF16)
    z = jnp.dot(h, win_ref[...], preferred_element_type=F32)

    cqn = _rms(z[:, CQ0:CKV0], gq_ref[...]).astype(BF16)
    q = jnp.dot(cqn, wuq_ref[...], preferred_element_type=F32)
    cq = cq_ref[...]
    sq = sq_ref[...]
    for hd in range(HEADS):
        qh = q[:, hd * HEAD_PAD:(hd + 1) * HEAD_PAD]
        q_ref[0, hd] = (qh * cq + _swap_rope_halves(qh) * sq).astype(BF16)

    ckv_ref[0] = _rms(z[:, CKV0:KPE0], gkv_ref[...])

    kp = z[:, KPE0:UC0]
    kp = kp * ck_ref[...] + _swap_rope_halves(kp) * sk_ref[...]
    kpe_ref[0] = kp[:, ROPE_LO:ROPE_LO + QK_ROPE]

    u = z[:, CG0:ZW] * z[:, UC0:BG0]

    @pl.when(si == 0)
    def _():
        carry_ref[0:2, :] = past_ref[0]

    prev2 = carry_ref[0:1, :]
    prev1 = carry_ref[1:2, :]
    row = lax.broadcasted_iota(I32, u.shape, 0)
    u1 = jnp.where(row == 0, prev1, pltpu.roll(u, 1, 0))
    u2 = jnp.where(row == 0, prev2, jnp.where(row == 1, prev1, pltpu.roll(u, 2, 0)))
    conv = u2 * wconv_ref[0:1, :] + u1 * wconv_ref[1:2, :] + u * wconv_ref[2:3, :]
    convn_ref[0] = _rms(z[:, BG0:CG0] * conv, gconv_ref[...]).astype(BF16)
    carry_ref[0:2, :] = u[ts - 2:ts, :]

    @pl.when(si == nc_tile)
    def _():
        newconv_ref[0] = u[nc_row:nc_row + 2, :]


def _front(x, past_conv, tabs, w, *, ts, s_valid):
    b, s, d = x.shape
    cq, sq, ck, sk = tabs
    full = lambda a: pl.BlockSpec(a.shape, lambda i, j: (0,) * a.ndim)
    tab = pl.BlockSpec((ts, LANES), lambda i, j: (j, 0))
    kern = functools.partial(_front_kernel, ts=ts, nc_tile=(s_valid - 2) // ts,
                             nc_row=(s_valid - 2) % ts)
    return pl.pallas_call(
        kern,
        grid=(b, s // ts),
        in_specs=[pl.BlockSpec((1, ts, d), lambda i, j: (i, j, 0)),
                  full(w["w_in"]), full(w["g_attn"]), full(w["g_q"]), full(w["w_uq"]),
                  full(w["g_kv"]), full(w["w_conv"]), full(w["g_conv"]),
                  tab, tab, tab, tab,
                  pl.BlockSpec((1, 2, CONV_WIDTH), lambda i, j: (i, 0, 0))],
        out_specs=[pl.BlockSpec((1, HEADS, ts, HEAD_PAD), lambda i, j: (i, 0, j, 0)),
                   pl.BlockSpec((1, ts, KV_RANK), lambda i, j: (i, j, 0)),
                   pl.BlockSpec((1, ts, QK_ROPE), lambda i, j: (i, j, 0)),
                   pl.BlockSpec((1, ts, CONV_WIDTH), lambda i, j: (i, j, 0)),
                   pl.BlockSpec((1, 2, CONV_WIDTH), lambda i, j: (i, 0, 0))],
        out_shape=[jax.ShapeDtypeStruct((b, HEADS, s, HEAD_PAD), BF16),
                   jax.ShapeDtypeStruct((b, s, KV_RANK), F32),
                   jax.ShapeDtypeStruct((b, s, QK_ROPE), F32),
                   jax.ShapeDtypeStruct((b, s, CONV_WIDTH), BF16),
                   jax.ShapeDtypeStruct((b, 2, CONV_WIDTH), F32)],
        scratch_shapes=[pltpu.VMEM((SUBLANES, CONV_WIDTH), F32)],
        compiler_params=_params(48, 2),
        name="front",
    )(x, w["w_in"], w["g_attn"], w["g_q"], w["w_uq"], w["g_kv"], w["w_conv"], w["g_conv"],
      cq, sq, ck, sk, past_conv)


def _kvup_kernel(ckv_ref, kpe_ref, wk_ref, wvt_ref, place_ref, k_ref, vt_ref):
    c = ckv_ref[0].astype(BF16)
    kn = jnp.dot(c, wk_ref[...], preferred_element_type=F32)
    kp = jnp.dot(kpe_ref[0].astype(BF16), place_ref[...], preferred_element_type=F32)
    for hd in range(HEADS):
        k_ref[0, hd] = (kn[:, hd * HEAD_PAD:(hd + 1) * HEAD_PAD] + kp).astype(BF16)
    vt = lax.dot_general(wvt_ref[...], c, (((1,), (1,)), ((), ())),
                         preferred_element_type=F32)
    ones_row = jnp.where(lax.broadcasted_iota(I32, (VT_ROWS - V_DIM, vt.shape[1]), 0) == 0,
                         1.0, 0.0).astype(BF16)
    for hd in range(HEADS):
        vt_ref[0, hd, 0, 0:V_DIM, :] = vt[hd * V_DIM:(hd + 1) * V_DIM, :].astype(BF16)
        vt_ref[0, hd, 0, V_DIM:VT_ROWS, :] = ones_row


def _kv_up(ckv, kpe, w, *, tk):
    b, sk, _ = ckv.shape
    nkb = sk // tk
    full = lambda a: pl.BlockSpec(a.shape, lambda i, j: (0,) * a.ndim)
    return pl.pallas_call(
        _kvup_kernel,
        grid=(b, nkb),
        in_specs=[pl.BlockSpec((1, tk, KV_RANK), lambda i, j: (i, j, 0)),
                  pl.BlockSpec((1, tk, QK_ROPE), lambda i, j: (i, j, 0)),
                  full(w["w_k"]), full(w["w_vt"]), full(w["place"])],
        out_specs=[pl.BlockSpec((1, HEADS, tk, HEAD_PAD), lambda i, j: (i, 0, j, 0)),
                   pl.BlockSpec((1, HEADS, 1, VT_ROWS, tk), lambda i, j: (i, 0, j, 0, 0))],
        out_shape=[jax.ShapeDtypeStruct((b, HEADS, sk, HEAD_PAD), BF16),
                   jax.ShapeDtypeStruct((b, HEADS, nkb, VT_ROWS, tk), BF16)],
        compiler_params=_params(32, 2),
        name="kv_up",
    )(ckv, kpe, w["w_k"], w["w_vt"], w["place"])


def _attn_kernel(q_ref, k_ref, vt_ref, o_ref, sa_ref, sb_ref,
                 *, tq, tk, nq, nkb, q_off, sk_valid):
    def q_block(qi, _):
        q = q_ref[0, 0, pl.ds(pl.multiple_of(qi * tq, tq), tq), :]
        q0 = q_off + qi * tq
        qch = lax.shift_right_logical(q0 + lax.broadcasted_iota(I32, (1, tq), 1), CHUNK_SHIFT)
        last_visible = lax.shift_left(lax.shift_right_logical(q0 + tq - 1, CHUNK_SHIFT) + 1,
                                      CHUNK_SHIFT) - 1
        nb = jnp.minimum(nkb, last_visible // tk + 1)
        first_chunk_end = lax.shift_left(lax.shift_right_logical(q0, CHUNK_SHIFT) + 1,
                                         CHUNK_SHIFT)
        n_full = jnp.minimum(jnp.minimum(first_chunk_end // tk, sk_valid // tk), nb)

        def scores(kb):
            k = k_ref[0, 0, pl.ds(pl.multiple_of(kb * tk, tk), tk), :]
            return lax.dot_general(k, q, (((1,), (1,)), ((), ())), preferred_element_type=F32)

        def put_scores(s_ref, kb):
            s = scores(kb)
            s_ref[...] = s
            return jnp.max(s, axis=0, keepdims=True)

        def update(s_ref, block_max, kb, carry):
            m, acc = carry
            m_new = jnp.maximum(m, block_max)
            p = jnp.exp2(s_ref[...] - m_new).astype(BF16)
            acc = jnp.exp2(m - m_new) * acc + jnp.dot(vt_ref[0, 0, kb], p,
                                                      preferred_element_type=F32)
            return m_new, acc

        def pair(j, carry):
            m, acc, max_a = carry
            max_b = put_scores(sb_ref, 2 * j + 1)
            m, acc = update(sa_ref, max_a, 2 * j, (m, acc))
            max_a = put_scores(sa_ref, jnp.minimum(2 * j + 2, nkb - 1))
            m, acc = update(sb_ref, max_b, 2 * j + 1, (m, acc))
            return m, acc, max_a

        def masked_update(s_ref, kb, carry):
            kpos = kb * tk + lax.broadcasted_iota(I32, (tk, 1), 0)
            vis = (lax.shift_right_logical(kpos, CHUNK_SHIFT) <= qch) & (kpos < sk_valid)
            s = jnp.where(vis, s_ref[...], NEG_INF)
            s_ref[...] = s
            return update(s_ref, jnp.max(s, axis=0, keepdims=True), kb, carry)

        n_pairs = n_full // 2
        base = 2 * n_pairs
        rest = nb - base

        def tail_one(_, carry):
            return masked_update(sa_ref, base, carry)

        def tail_two(_, carry):
            sb_ref[...] = scores(base + 1)
            carry = masked_update(sa_ref, base, carry)
            return masked_update(sb_ref, base + 1, carry)

        def tail_more(kb, carry):
            sa_ref[...] = scores(kb)
            return masked_update(sa_ref, kb, carry)

        max_0 = put_scores(sa_ref, 0)
        carry = (jnp.full((1, tq), NEG_INF, F32), jnp.zeros((VT_ROWS, tq), F32), max_0)
        m, acc, _ = lax.fori_loop(0, n_pairs, pair, carry)
        carry = (m, acc)
        carry = lax.fori_loop(0, (rest == 1).astype(I32), tail_one, carry)
        carry = lax.fori_loop(0, (rest >= 2).astype(I32), tail_two, carry)
        _, acc = lax.fori_loop(base + 2, nb, tail_more, carry)
        o_ref[0, 0, qi] = acc[:V_DIM] / acc[V_DIM:V_DIM + 1]
        return 0

    lax.fori_loop(0, nq, q_block, 0)


def _attention(q, k, vt, *, tq, q_off, sk_valid):
    b, _, s, _ = q.shape
    _, _, nkb, _, tk = vt.shape
    sk = k.shape[2]
    nq = s // tq
    kern = functools.partial(_attn_kernel, tq=tq, tk=tk, nq=nq, nkb=nkb, q_off=q_off,
                             sk_valid=sk_valid)
    return pl.pallas_call(
        kern,
        grid=(b, HEADS),
        in_specs=[pl.BlockSpec((1, 1, s, HEAD_PAD), lambda i, h: (i, h, 0, 0)),
                  pl.BlockSpec((1, 1, sk, HEAD_PAD), lambda i, h: (i, h, 0, 0)),
                  pl.BlockSpec((1, 1, nkb, VT_ROWS, tk), lambda i, h: (i, h, 0, 0, 0))],
        out_specs=pl.BlockSpec((1, 1, nq, V_DIM, tq), lambda i, h: (i, h, 0, 0, 0)),
        out_shape=jax.ShapeDtypeStruct((b, HEADS, nq, V_DIM, tq), F32),
        scratch_shapes=[pltpu.VMEM((tk, tq), F32), pltpu.VMEM((tk, tq), F32)],
        compiler_params=_params(32, 2),
        name="attn",
    )(q, k, vt)


def _merge_kernel(x_ref, at_ref, cn_ref, gm_ref, woa_ref, woc_ref, gf_ref, x1_ref, hn_ref, *, ts):
    a = at_ref[0, :, 0].reshape(HEADS * V_DIM, ts).T
    an = _rms(a, gm_ref[...]).astype(BF16)
    mix = (jnp.dot(an, woa_ref[...], preferred_element_type=F32)
           + jnp.dot(cn_ref[0], woc_ref[...], preferred_element_type=F32))
    x1 = x_ref[0] + mix
    x1_ref[0] = x1
    hn_ref[0] = _rms(x1, gf_ref[...]).astype(BF16)


def _merge(x, attn_t, convn, w, *, ts):
    b, s, d = x.shape
    assert attn_t.shape[4] == ts, "attention query blocks and merge row tiles must coincide"
    full = lambda a: pl.BlockSpec(a.shape, lambda i, j: (0,) * a.ndim)
    return pl.pallas_call(
        functools.partial(_merge_kernel, ts=ts),
        grid=(b, s // ts),
        in_specs=[pl.BlockSpec((1, ts, d), lambda i, j: (i, j, 0)),
                  pl.BlockSpec((1, HEADS, 1, V_DIM, ts), lambda i, j: (i, 0, j, 0, 0)),
                  pl.BlockSpec((1, ts, CONV_WIDTH), lambda i, j: (i, j, 0)),
                  full(w["g_mla"]), full(w["w_o_attn"]), full(w["w_o_conv"]), full(w["g_ffn"])],
        out_specs=[pl.BlockSpec((1, ts, d), lambda i, j: (i, j, 0)),
                   pl.BlockSpec((1, ts, d), lambda i, j: (i, j, 0))],
        out_shape=[jax.ShapeDtypeStruct((b, s, d), F32), jax.ShapeDtypeStruct((b, s, d), BF16)],
        compiler_params=_params(32, 2),
        name="merge",
    )(x, attn_t, convn, w["g_mla"], w["w_o_attn"], w["w_o_conv"], w["g_ffn"])


def _route_kernel(hn_ref, wq_ref, keys_ref, i_out, j_out, g_out,
                  qt_ref, sv_ref, si_ref, cand_ref, ce_ref, eb_ref, gb_ref, *, tr):
    n_half = 2 * PEER_HEADS
    qt = lax.dot_general(wq_ref[...], hn_ref[...], (((1,), (1,)), ((), ())),
                         preferred_element_type=F32)
    qt_ref[...] = qt.astype(BF16).reshape(n_half, N_KEYS, tr)

    def half_body(hc, _):
        s = jnp.dot(keys_ref[hc], qt_ref[hc], preferred_element_type=F32)
        key_id = lax.broadcasted_iota(I32, (N_KEYS, tr), 0).astype(F32)
        for k in range(PEER_TOPK):
            m = jnp.max(s, axis=0, keepdims=True)
            idx = jnp.min(jnp.where(s == m, key_id, float(N_KEYS)), axis=0, keepdims=True)
            sv_ref[hc, k:k + 1, :] = m
            si_ref[hc, k:k + 1, :] = idx
            s = jnp.where(key_id == idx, -jnp.inf, s)
        return 0

    lax.fori_loop(0, n_half, half_body, 0, unroll=2)

    cand_ref[N_CAND:N_CAND_PAD, :] = jnp.full((N_CAND_PAD - N_CAND, tr), -jnp.inf, F32)
    ce_ref[N_CAND:N_CAND_PAD, :] = jnp.zeros((N_CAND_PAD - N_CAND, tr), F32)

    def head_body(hd, _):
        for c, (a, b) in enumerate(CANDS):
            cand_ref[c:c + 1, :] = sv_ref[2 * hd, a:a + 1, :] + sv_ref[2 * hd + 1, b:b + 1, :]
            ce_ref[c:c + 1, :] = (si_ref[2 * hd, a:a + 1, :] * float(N_KEYS)
                                  + si_ref[2 * hd + 1, b:b + 1, :])
        cand = cand_ref[...]
        ce = ce_ref[...]
        slot = lax.broadcasted_iota(I32, (N_CAND_PAD, tr), 0).astype(F32)
        for k in range(PEER_TOPK):
            m = jnp.max(cand, axis=0, keepdims=True)
            idx = jnp.min(jnp.where(cand == m, slot, float(N_CAND_PAD)), axis=0, keepdims=True)
            sel = slot == idx
            gb_ref[hd, k:k + 1, :] = m
            eb_ref[hd, k:k + 1, :] = jnp.max(jnp.where(sel, ce, -1.0), axis=0, keepdims=True)
            cand = jnp.where(sel, -jnp.inf, cand)
        best = gb_ref[hd]
        e = jnp.exp(best - best[0:1, :])
        gb_ref[hd] = e / jnp.sum(e, axis=0, keepdims=True)
        return 0

    lax.fori_loop(0, PEER_HEADS, head_body, 0, unroll=2)

    n_pick = PEER_HEADS * PEER_TOPK
    experts = eb_ref[...].reshape(n_pick, tr).T.astype(I32)
    i_out[...] = lax.shift_right_logical(experts, 7)
    j_out[...] = experts & (N_KEYS - 1)
    g_out[...] = gb_ref[...].reshape(n_pick, tr).T


def _route(hn, w, *, tr):
    t, d = hn.shape
    n_half = 2 * PEER_HEADS
    n_pick = PEER_HEADS * PEER_TOPK
    full = lambda a: pl.BlockSpec(a.shape, lambda i: (0,) * a.ndim)
    pick = pl.BlockSpec((tr, n_pick), lambda i: (i, 0))
    return pl.pallas_call(
        functools.partial(_route_kernel, tr=tr),
        grid=(t // tr,),
        in_specs=[pl.BlockSpec((tr, d), lambda i: (i, 0)), full(w["w_pq_t"]), full(w["keys"])],
        out_specs=[pick, pick, pick],
        out_shape=[jax.ShapeDtypeStruct((t, n_pick), I32), jax.ShapeDtypeStruct((t, n_pick), I32),
                   jax.ShapeDtypeStruct((t, n_pick), F32)],
        scratch_shapes=[pltpu.VMEM((n_half, N_KEYS, tr), BF16),
                        pltpu.VMEM((n_half, PEER_TOPK, tr), F32),
                        pltpu.VMEM((n_half, PEER_TOPK, tr), F32),
                        pltpu.VMEM((N_CAND_PAD, tr), F32),
                        pltpu.VMEM((N_CAND_PAD, tr), F32),
                        pltpu.VMEM((PEER_HEADS, PEER_TOPK, tr), F32),
                        pltpu.VMEM((PEER_HEADS, PEER_TOPK, tr), F32)],
        compiler_params=_params(32, 1),
        name="route",
    )(hn, w["w_pq_t"], w["keys"])


def _expert_kernel(hn_ref, x1_ref, i_ref, j_ref, g_ref, u_ref, v_ref, gfin_ref, y_ref,
                   w_ref, acc_ref, *, tt, ni, n_et):
    ne = pl.program_id(1)

    @pl.when(ne == 0)
    def _():
        acc_ref[...] = jnp.zeros_like(acc_ref)
        key_id = lax.broadcasted_iota(I32, (N_KEYS, N_KEYS), 0)
        zero = jnp.zeros((N_KEYS, N_KEYS), BF16)

        def one_hots(t):
            irow = i_ref[pl.ds(t, 1), :]
            jrow = j_ref[pl.ds(t, 1), :]
            grow = g_ref[pl.ds(t, 1), :]
            at = jnp.where(key_id == irow, grow, 0.0).astype(BF16)
            bt = jnp.where(key_id == jrow, 1.0, 0.0).astype(BF16)
            return at, bt

        def token_pair(tp, _):
            at0, bt0 = one_hots(tp)
            at1, bt1 = one_hots(tp + tt // 2)
            lhs = jnp.concatenate([at0, at1], axis=1)
            rhs = jnp.concatenate([jnp.concatenate([bt0, zero], axis=1),
                                   jnp.concatenate([zero, bt1], axis=1)], axis=0)
            w = lax.dot_general(lhs, rhs, (((1,), (1,)), ((), ())), preferred_element_type=F32)
            lo = lax.bitcast_convert_type(w[:, :N_KEYS].astype(BF16).astype(F32), U32)
            hi = lax.bitcast_convert_type(w[:, N_KEYS:].astype(BF16).astype(F32), U32)
            row0 = pl.multiple_of(tp * W_PITCH, SUBLANES)
            w_ref[pl.ds(row0, N_KEYS), :] = hi | lax.shift_right_logical(lo, jnp.uint32(16))
            return 0

        lax.fori_loop(0, tt // 2, token_pair, 0, unroll=32)

    a = lax.dot_general(hn_ref[...], u_ref[...], (((1,), (1,)), ((), ())),
                        preferred_element_type=F32)
    act = 0.5 * a * (1.0 + lax.erf(a * np.float32(np.sqrt(0.5))))

    def gate_columns(ii):
        words = w_ref[pl.ds(ne * ni + ii, tt // 2, stride=W_PITCH), :]
        first = lax.bitcast_convert_type(lax.shift_left(words, jnp.uint32(16)), F32)
        second = lax.bitcast_convert_type(words & jnp.uint32(0xFFFF0000), F32)
        return jnp.concatenate([first, second], axis=0)

    w = jnp.concatenate([gate_columns(ii) for ii in range(ni)], axis=1)
    acc_ref[...] += jnp.dot((w * act).astype(BF16), v_ref[...], preferred_element_type=F32)

    @pl.when(ne == n_et - 1)
    def _():
        y_ref[...] = _rms(x1_ref[...] + acc_ref[...], gfin_ref[...])


def _experts(hn, x1, pick_i, pick_j, pick_g, w, g_final, *, tt, ni):
    t, d = hn.shape
    n_exp = w["u"].shape[0]
    et = ni * N_KEYS
    n_et = n_exp // et
    n_pick = PEER_HEADS * PEER_TOPK
    tok = lambda width: pl.BlockSpec((tt, width), lambda i, e: (i, 0))
    return pl.pallas_call(
        functools.partial(_expert_kernel, tt=tt, ni=ni, n_et=n_et),
        grid=(t // tt, n_et),
        in_specs=[tok(d), tok(d), tok(n_pick), tok(n_pick), tok(n_pick),
                  pl.BlockSpec((et, d), lambda i, e: (e, 0)),
                  pl.BlockSpec((et, d), lambda i, e: (e, 0)),
                  pl.BlockSpec((1, d), lambda i, e: (0, 0))],
        out_specs=tok(d),
        out_shape=jax.ShapeDtypeStruct((t, d), F32),
        scratch_shapes=[pltpu.VMEM((tt // 2 * W_PITCH, N_KEYS), U32), pltpu.VMEM((tt, d), F32)],
        compiler_params=_params(56, 2),
        name="experts",
    )(hn, x1, pick_i, pick_j, pick_g, w["u"], w["v"], g_final)


def _rope_tables(pos, scale):
    freqs = ROPE_THETA ** (-jnp.arange(ROPE_HALF, dtype=F32) / ROPE_HALF)
    ang = pos.astype(F32)[:, None] * freqs[None, :]
    cos, sin = jnp.cos(ang), jnp.sin(ang)
    n = pos.shape[0]
    ones = jnp.ones((n, QK_NOPE), F32)
    tail = jnp.ones((n, HEAD_PAD - QK_NOPE - QK_ROPE), F32)
    cos_tab = jnp.concatenate([ones, cos, cos, tail], axis=1)
    sin_tab = jnp.concatenate([0 * ones, -sin, sin, 0 * tail], axis=1)
    return cos_tab * scale, sin_tab * scale, cos_tab, sin_tab


def _prep_weights(l, g_attn_norm, w_in, g_q, w_uq, g_kv, w_ukv, w_conv, g_mla_out, g_conv_out,
                  w_out, g_ffn_norm, w_peer_q, peer_keys, peer_u, peer_v):
    d = w_in.shape[1]
    zeros = lambda n: jnp.zeros((d, n), w_in.dtype)
    kpe_end = KV_RANK + Q_RANK + QK_ROPE
    w_in_r = jnp.concatenate(
        [w_in[l][:, :CKV0 + KV_RANK], zeros(ROPE_LO), w_in[l][:, CKV0 + KV_RANK:kpe_end],
         zeros(HEAD_PAD - ROPE_LO - QK_ROPE), w_in[l][:, kpe_end:]], axis=1)
    w_uq_r = jnp.pad(w_uq[l].reshape(Q_RANK, HEADS, QK_NOPE + QK_ROPE),
                     ((0, 0), (0, 0), (0, HEAD_PAD - QK_NOPE - QK_ROPE)))
    kv3 = w_ukv[l].reshape(KV_RANK, HEADS, QK_NOPE + V_DIM)
    w_k = jnp.pad(kv3[:, :, :QK_NOPE], ((0, 0), (0, 0), (0, HEAD_PAD - QK_NOPE)))
    w_vt = kv3[:, :, QK_NOPE:].reshape(KV_RANK, HEADS * V_DIM).T
    place = jnp.zeros((QK_ROPE, HEAD_PAD), F32).at[
        jnp.arange(QK_ROPE), ROPE_LO + jnp.arange(QK_ROPE)].set(1.0)
    row = lambda g: g[l].reshape(1, -1).astype(F32)
    mla_w = HEADS * V_DIM
    return {
        "w_in": w_in_r.astype(BF16), "g_attn": row(g_attn_norm), "g_q": row(g_q),
        "w_uq": w_uq_r.reshape(Q_RANK, HEADS * HEAD_PAD).astype(BF16), "g_kv": row(g_kv),
        "w_conv": w_conv[l].astype(F32), "g_conv": row(g_conv_out),
        "w_k": w_k.reshape(KV_RANK, HEADS * HEAD_PAD).astype(BF16), "w_vt": w_vt.astype(BF16),
        "place": place.astype(BF16),
        "g_mla": row(g_mla_out), "w_o_attn": w_out[l][:mla_w].astype(BF16),
        "w_o_conv": w_out[l][mla_w:].astype(BF16), "g_ffn": row(g_ffn_norm),
        "w_pq_t": w_peer_q[l].T.astype(BF16),
        "keys": peer_keys[l].reshape(2 * PEER_HEADS, N_KEYS, -1).astype(BF16),
        "u": peer_u[l].astype(BF16), "v": peer_v[l].astype(BF16),
    }


def _tile(n, pref):
    t = min(n, pref)
    assert n % t == 0, (n, pref)
    return t


def _layer(x, pos0, s_valid, past_kv, past_kpe, past_conv, w, g_final):
    b, s, d = x.shape
    scale = float(QK_NOPE + QK_ROPE) ** -0.5 * float(np.log2(np.e))
    tabs = _rope_tables(pos0 + jnp.arange(s, dtype=jnp.int32), scale)
    if past_conv is None:
        past_conv = jnp.zeros((b, 2, CONV_WIDTH), F32)
    q, ckv, kpe, convn, new_conv = _front(x, past_conv, tabs, w, ts=_tile(s, 512), s_valid=s_valid)

    if past_kv is None:
        ckv_all, kpe_all, sk_valid = ckv, kpe, s
    else:
        ckv_all = jnp.concatenate([past_kv, ckv[:, :s_valid]], axis=1)
        kpe_all = jnp.concatenate([past_kpe, kpe[:, :s_valid]], axis=1)
        sk_valid = ckv_all.shape[1]
    tk = 512 if sk_valid % 512 == 0 else 256
    pad = -sk_valid % tk
    if pad:
        ckv_all = jnp.pad(ckv_all, ((0, 0), (0, pad), (0, 0)))
        kpe_all = jnp.pad(kpe_all, ((0, 0), (0, pad), (0, 0)))
    k, vt = _kv_up(ckv_all, kpe_all, w, tk=tk)
    attn_t = _attention(q, k, vt, tq=_tile(s, 512), q_off=pos0, sk_valid=sk_valid)
    x1, hn = _merge(x, attn_t, convn, w, ts=_tile(s, 512))

    t = b * s_valid
    hn2 = hn[:, :s_valid].reshape(t, d)
    x12 = x1[:, :s_valid].reshape(t, d)
    pick_i, pick_j, pick_g = _route(hn2, w, tr=_tile(t, 512))
    y = _experts(hn2, x12, pick_i, pick_j, pick_g, w, g_final, tt=_tile(t, 512), ni=16)
    return y.reshape(b, s_valid, d), ckv, kpe, new_conv


def kernel(x_prompt, x_sample, cache_kv_latent, cache_k_rope, state_conv, g_attn_norm, w_in, g_q, w_uq, g_kv, w_ukv, w_conv, g_mla_out, g_conv_out, w_out, g_ffn_norm, w_peer_q, peer_keys, peer_u, peer_v, g_final):
    depth = w_in.shape[0]
    assert depth == 1, "the final norm is fused into the last layer; one layer supported"
    past_len = cache_kv_latent.shape[2]
    s_dec = x_sample.shape[1]
    s_pad = -(-s_dec // LANES) * LANES
    hs = jnp.pad(x_sample, ((0, 0), (0, s_pad - s_dec), (0, 0)))
    gfin = g_final.reshape(1, -1).astype(F32)
    w = _prep_weights(0, g_attn_norm, w_in, g_q, w_uq, g_kv, w_ukv, w_conv, g_mla_out, g_conv_out,
                      w_out, g_ffn_norm, w_peer_q, peer_keys, peer_u, peer_v)
    yp, kv_p, kpe_p, conv_p = _layer(x_prompt, 0, x_prompt.shape[1], None, None, None, w, gfin)
    ys, kv_s, kpe_s, conv_s = _layer(hs, past_len, s_dec, cache_kv_latent[0], cache_k_rope[0],
                                     state_conv[0], w, gfin)
    return (yp, ys[:, :s_dec],
            kv_p[None], kpe_p[None], conv_p[None],
            kv_s[None, :, :s_dec], kpe_s[None, :, :s_dec], conv_s[None])
```

```python
import functools

import numpy as np
import jax
import jax.numpy as jnp
from jax import lax
from jax.experimental import pallas as pl
from jax.experimental.pallas import tpu as pltpu

F32 = jnp.float32
BF16 = jnp.bfloat16
I32 = jnp.int32
U32 = jnp.uint32

EPS = 1e-6
NEG_INF = -1e30
CHUNK_SHIFT = 6
HEADS = 8
QK_NOPE = 64
QK_ROPE = 32
V_DIM = 64
VT_ROWS = 80
Q_RANK = 384
KV_RANK = 256
CONV_WIDTH = 512
ROPE_THETA = 10000.0
HEAD_PAD = 128
ROPE_LO = QK_NOPE
ROPE_HALF = QK_ROPE // 2
PEER_HEADS = 8
N_KEYS = 128
PEER_TOPK = 16
LANES = 128
SUBLANES = 8
W_PITCH = 136

CQ0, CKV0, KPE0, UC0, BG0, CG0, ZW = 0, 384, 640, 768, 1280, 1792, 2304

CANDS = sorted(
    [(a, b) for a in range(PEER_TOPK) for b in range(PEER_TOPK) if (a + 1) * (b + 1) <= PEER_TOPK],
    key=lambda ab: ab[0] * PEER_TOPK + ab[1])
N_CAND = len(CANDS)
N_CAND_PAD = -(-N_CAND // SUBLANES) * SUBLANES


def _rms(x, g):
    ms = jnp.mean(x * x, axis=-1, keepdims=True)
    return x * lax.rsqrt(ms + EPS) * g


def _swap_rope_halves(v):
    lane = lax.broadcasted_iota(I32, v.shape, 1)
    return jnp.where(lane < ROPE_LO + ROPE_HALF,
                     pltpu.roll(v, LANES - ROPE_HALF, 1), pltpu.roll(v, ROPE_HALF, 1))


def _params(vmem_mb, n_axes):
    return pltpu.CompilerParams(dimension_semantics=("arbitrary",) * n_axes,
                                vmem_limit_bytes=vmem_mb * 1024 * 1024)


def _front_kernel(x_ref, win_ref, gattn_ref, gq_ref, wuq_ref, gkv_ref, wconv_ref, gconv_ref,
                  cq_ref, sq_ref, ck_ref, sk_ref, past_ref,
                  q_ref, ckv_ref, kpe_ref, convn_ref, newconv_ref, carry_ref,
                  *, ts, nc_tile, nc_row):
    si = pl.program_id(1)
    x = x_ref[0]
    h = _rms(x, gattn_ref[...]).astype(BF16)
    z = jnp.dot(h, win_ref[...], preferred_element_type=F32)

    cqn = _rms(z[:, CQ0:CKV0], gq_ref[...]).astype(BF16)
    q = jnp.dot(cqn, wuq_ref[...], preferred_element_type=F32)
    cq = cq_ref[...]
    sq = sq_ref[...]
    for hd in range(HEADS):
        qh = q[:, hd * HEAD_PAD:(hd + 1) * HEAD_PAD]
        q_ref[0, hd] = (qh * cq + _swap_rope_halves(qh) * sq).astype(BF16)

    ckv_ref[0] = _rms(z[:, CKV0:KPE0], gkv_ref[...])

    kp = z[:, KPE0:UC0]
    kp = kp * ck_ref[...] + _swap_rope_halves(kp) * sk_ref[...]
    kpe_ref[0] = kp[:, ROPE_LO:ROPE_LO + QK_ROPE]

    u = z[:, CG0:ZW] * z[:, UC0:BG0]

    @pl.when(si == 0)
    def _():
        carry_ref[0:2, :] = past_ref[0]

    prev2 = carry_ref[0:1, :]
    prev1 = carry_ref[1:2, :]
    row = lax.broadcasted_iota(I32, u.shape, 0)
    u1 = jnp.where(row == 0, prev1, pltpu.roll(u, 1, 0))
    u2 = jnp.where(row == 0, prev2, jnp.where(row == 1, prev1, pltpu.roll(u, 2, 0)))
    conv = u2 * wconv_ref[0:1, :] + u1 * wconv_ref[1:2, :] + u * wconv_ref[2:3, :]
    convn_ref[0] = _rms(z[:, BG0:CG0] * conv, gconv_ref[...]).astype(BF16)
    carry_ref[0:2, :] = u[ts - 2:ts, :]

    @pl.when(si == nc_tile)
    def _():
        newconv_ref[0] = u[nc_row:nc_row + 2, :]


def _front(x, past_conv, tabs, w, *, ts, s_valid):
    b, s, d = x.shape
    cq, sq, ck, sk = tabs
    full = lambda a: pl.BlockSpec(a.shape, lambda i, j: (0,) * a.ndim)
    tab = pl.BlockSpec((ts, LANES), lambda i, j: (j, 0))
    kern = functools.partial(_front_kernel, ts=ts, nc_tile=(s_valid - 2) // ts,
                             nc_row=(s_valid - 2) % ts)
    return pl.pallas_call(
        kern,
        grid=(b, s // ts),
        in_specs=[pl.BlockSpec((1, ts, d), lambda i, j: (i, j, 0)),
                  full(w["w_in"]), full(w["g_attn"]), full(w["g_q"]), full(w["w_uq"]),
                  full(w["g_kv"]), full(w["w_conv"]), full(w["g_conv"]),
                  tab, tab, tab, tab,
                  pl.BlockSpec((1, 2, CONV_WIDTH), lambda i, j: (i, 0, 0))],
        out_specs=[pl.BlockSpec((1, HEADS, ts, HEAD_PAD), lambda i, j: (i, 0, j, 0)),
                   pl.BlockSpec((1, ts, KV_RANK), lambda i, j: (i, j, 0)),
                   pl.BlockSpec((1, ts, QK_ROPE), lambda i, j: (i, j, 0)),
                   pl.BlockSpec((1, ts, CONV_WIDTH), lambda i, j: (i, j, 0)),
                   pl.BlockSpec((1, 2, CONV_WIDTH), lambda i, j: (i, 0, 0))],
        out_shape=[jax.ShapeDtypeStruct((b, HEADS, s, HEAD_PAD), BF16),
                   jax.ShapeDtypeStruct((b, s, KV_RANK), F32),
                   jax.ShapeDtypeStruct((b, s, QK_ROPE), F32),
                   jax.ShapeDtypeStruct((b, s, CONV_WIDTH), BF16),
                   jax.ShapeDtypeStruct((b, 2, CONV_WIDTH), F32)],
        scratch_shapes=[pltpu.VMEM((SUBLANES, CONV_WIDTH), F32)],
        compiler_params=_params(48, 2),
        name="front",
    )(x, w["w_in"], w["g_attn"], w["g_q"], w["w_uq"], w["g_kv"], w["w_conv"], w["g_conv"],
      cq, sq, ck, sk, past_conv)


def _kvup_kernel(ckv_ref, kpe_ref, wk_ref, wvt_ref, place_ref, k_ref, vt_ref):
    c = ckv_ref[0].astype(BF16)
    kn = jnp.dot(c, wk_ref[...], preferred_element_type=F32)
    kp = jnp.dot(kpe_ref[0].astype(BF16), place_ref[...], preferred_element_type=F32)
    for hd in range(HEADS):
        k_ref[0, hd] = (kn[:, hd * HEAD_PAD:(hd + 1) * HEAD_PAD] + kp).astype(BF16)
    vt = lax.dot_general(wvt_ref[...], c, (((1,), (1,)), ((), ())),
                         preferred_element_type=F32)
    ones_row = jnp.where(lax.broadcasted_iota(I32, (VT_ROWS - V_DIM, vt.shape[1]), 0) == 0,
                         1.0, 0.0).astype(BF16)
    for hd in range(HEADS):
        vt_ref[0, hd, 0, 0:V_DIM, :] = vt[hd * V_DIM:(hd + 1) * V_DIM, :].astype(BF16)
        vt_ref[0, hd, 0, V_DIM:VT_ROWS, :] = ones_row


def _kv_up(ckv, kpe, w, *, tk):
    b, sk, _ = ckv.shape
    nkb = sk // tk
    full = lambda a: pl.BlockSpec(a.shape, lambda i, j: (0,) * a.ndim)
    return pl.pallas_call(
        _kvup_kernel,
        grid=(b, nkb),
        in_specs=[pl.BlockSpec((1, tk, KV_RANK), lambda i, j: (i, j, 0)),
                  pl.BlockSpec((1, tk, QK_ROPE), lambda i, j: (i, j, 0)),
                  full(w["w_k"]), full(w["w_vt"]), full(w["place"])],
        out_specs=[pl.BlockSpec((1, HEADS, tk, HEAD_PAD), lambda i, j: (i, 0, j, 0)),
                   pl.BlockSpec((1, HEADS, 1, VT_ROWS, tk), lambda i, j: (i, 0, j, 0, 0))],
        out_shape=[jax.ShapeDtypeStruct((b, HEADS, sk, HEAD_PAD), BF16),
                   jax.ShapeDtypeStruct((b, HEADS, nkb, VT_ROWS, tk), BF16)],
        compiler_params=_params(32, 2),
        name="kv_up",
    )(ckv, kpe, w["w_k"], w["w_vt"], w["place"])


def _attn_kernel(q_ref, k_ref, vt_ref, o_ref, sa_ref, sb_ref,
                 *, tq, tk, nq, nkb, q_off, sk_valid):
    def q_block(qi, _):
        q = q_ref[0, 0, pl.ds(pl.multiple_of(qi * tq, tq), tq), :]
        q0 = q_off + qi * tq
        qch = lax.shift_right_logical(q0 + lax.broadcasted_iota(I32, (1, tq), 1), CHUNK_SHIFT)
        last_visible = lax.shift_left(lax.shift_right_logical(q0 + tq - 1, CHUNK_SHIFT) + 1,
                                      CHUNK_SHIFT) - 1
        nb = jnp.minimum(nkb, last_visible // tk + 1)
        first_chunk_end = lax.shift_left(lax.shift_right_logical(q0, CHUNK_SHIFT) + 1,
                                         CHUNK_SHIFT)
        n_full = jnp.minimum(jnp.minimum(first_chunk_end // tk, sk_valid // tk), nb)

        def scores(kb):
            k = k_ref[0, 0, pl.ds(pl.multiple_of(kb * tk, tk), tk), :]
            return lax.dot_general(k, q, (((1,), (1,)), ((), ())), preferred_element_type=F32)

        def update(s_ref, kb, carry):
            m, acc = carry
            m_new = jnp.maximum(m, jnp.max(s_ref[...], axis=0, keepdims=True))
            p = jnp.exp2(s_ref[...] - m_new).astype(BF16)
            acc = jnp.exp2(m - m_new) * acc + jnp.dot(vt_ref[0, 0, kb], p,
                                                      preferred_element_type=F32)
            return m_new, acc

        def pair(j, carry):
            sb_ref[...] = scores(2 * j + 1)
            carry = update(sa_ref, 2 * j, carry)
            sa_ref[...] = scores(jnp.minimum(2 * j + 2, nkb - 1))
            return update(sb_ref, 2 * j + 1, carry)

        def masked_update(s_ref, kb, carry):
            kpos = kb * tk + lax.broadcasted_iota(I32, (tk, 1), 0)
            vis = (lax.shift_right_logical(kpos, CHUNK_SHIFT) <= qch) & (kpos < sk_valid)
            s_ref[...] = jnp.where(vis, s_ref[...], NEG_INF)
            return update(s_ref, kb, carry)

        n_pairs = n_full // 2
        base = 2 * n_pairs
        rest = nb - base

        def tail_one(_, carry):
            return masked_update(sa_ref, base, carry)

        def tail_two(_, carry):
            sb_ref[...] = scores(base + 1)
            carry = masked_update(sa_ref, base, carry)
            return masked_update(sb_ref, base + 1, carry)

        def tail_more(kb, carry):
            sa_ref[...] = scores(kb)
            return masked_update(sa_ref, kb, carry)

        sa_ref[...] = scores(0)
        carry = (jnp.full((1, tq), NEG_INF, F32), jnp.zeros((VT_ROWS, tq), F32))
        carry = lax.fori_loop(0, n_pairs, pair, carry)
        carry = lax.fori_loop(0, (rest == 1).astype(I32), tail_one, carry)
        carry = lax.fori_loop(0, (rest >= 2).astype(I32), tail_two, carry)
        _, acc = lax.fori_loop(base + 2, nb, tail_more, carry)
        o_ref[0, 0, qi] = acc[:V_DIM] / acc[V_DIM:V_DIM + 1]
        return 0

    lax.fori_loop(0, nq, q_block, 0)


def _attention(q, k, vt, *, tq, q_off, sk_valid):
    b, _, s, _ = q.shape
    _, _, nkb, _, tk = vt.shape
    sk = k.shape[2]
    nq = s // tq
    kern = functools.partial(_attn_kernel, tq=tq, tk=tk, nq=nq, nkb=nkb, q_off=q_off,
                             sk_valid=sk_valid)
    return pl.pallas_call(
        kern,
        grid=(b, HEADS),
        in_specs=[pl.BlockSpec((1, 1, s, HEAD_PAD), lambda i, h: (i, h, 0, 0)),
                  pl.BlockSpec((1, 1, sk, HEAD_PAD), lambda i, h: (i, h, 0, 0)),
                  pl.BlockSpec((1, 1, nkb, VT_ROWS, tk), lambda i, h: (i, h, 0, 0, 0))],
        out_specs=pl.BlockSpec((1, 1, nq, V_DIM, tq), lambda i, h: (i, h, 0, 0, 0)),
        out_shape=jax.ShapeDtypeStruct((b, HEADS, nq, V_DIM, tq), F32),
        scratch_shapes=[pltpu.VMEM((tk, tq), F32), pltpu.VMEM((tk, tq), F32)],
        compiler_params=_params(32, 2),
        name="attn",
    )(q, k, vt)


def _merge_kernel(x_ref, at_ref, cn_ref, gm_ref, woa_ref, woc_ref, gf_ref, x1_ref, hn_ref, *, ts):
    a = at_ref[0, :, 0].reshape(HEADS * V_DIM, ts).T
    an = _rms(a, gm_ref[...]).astype(BF16)
    mix = (jnp.dot(an, woa_ref[...], preferred_element_type=F32)
           + jnp.dot(cn_ref[0], woc_ref[...], preferred_element_type=F32))
    x1 = x_ref[0] + mix
    x1_ref[0] = x1
    hn_ref[0] = _rms(x1, gf_ref[...]).astype(BF16)


def _merge(x, attn_t, convn, w, *, ts):
    b, s, d = x.shape
    assert attn_t.shape[4] == ts, "attention query blocks and merge row tiles must coincide"
    full = lambda a: pl.BlockSpec(a.shape, lambda i, j: (0,) * a.ndim)
    return pl.pallas_call(
        functools.partial(_merge_kernel, ts=ts),
        grid=(b, s // ts),
        in_specs=[pl.BlockSpec((1, ts, d), lambda i, j: (i, j, 0)),
                  pl.BlockSpec((1, HEADS, 1, V_DIM, ts), lambda i, j: (i, 0, j, 0, 0)),
                  pl.BlockSpec((1, ts, CONV_WIDTH), lambda i, j: (i, j, 0)),
                  full(w["g_mla"]), full(w["w_o_attn"]), full(w["w_o_conv"]), full(w["g_ffn"])],
        out_specs=[pl.BlockSpec((1, ts, d), lambda i, j: (i, j, 0)),
                   pl.BlockSpec((1, ts, d), lambda i, j: (i, j, 0))],
        out_shape=[jax.ShapeDtypeStruct((b, s, d), F32), jax.ShapeDtypeStruct((b, s, d), BF16)],
        compiler_params=_params(32, 2),
        name="merge",
    )(x, attn_t, convn, w["g_mla"], w["w_o_attn"], w["w_o_conv"], w["g_ffn"])


def _route_kernel(hn_ref, wq_ref, keys_ref, i_out, j_out, g_out,
                  qt_ref, sv_ref, si_ref, cand_ref, ce_ref, eb_ref, gb_ref, *, tr):
    n_half = 2 * PEER_HEADS
    qt = lax.dot_general(wq_ref[...], hn_ref[...], (((1,), (1,)), ((), ())),
                         preferred_element_type=F32)
    qt_ref[...] = qt.astype(BF16).reshape(n_half, N_KEYS, tr)

    def half_body(hc, _):
        s = jnp.dot(keys_ref[hc], qt_ref[hc], preferred_element_type=F32)
        key_id = lax.broadcasted_iota(I32, (N_KEYS, tr), 0).astype(F32)
        for k in range(PEER_TOPK):
            m = jnp.max(s, axis=0, keepdims=True)
            idx = jnp.min(jnp.where(s == m, key_id, float(N_KEYS)), axis=0, keepdims=True)
            sv_ref[hc, k:k + 1, :] = m
            si_ref[hc, k:k + 1, :] = idx
            s = jnp.where(key_id == idx, -jnp.inf, s)
        return 0

    lax.fori_loop(0, n_half, half_body, 0, unroll=2)

    cand_ref[N_CAND:N_CAND_PAD, :] = jnp.full((N_CAND_PAD - N_CAND, tr), -jnp.inf, F32)
    ce_ref[N_CAND:N_CAND_PAD, :] = jnp.zeros((N_CAND_PAD - N_CAND, tr), F32)

    def head_body(hd, _):
        for c, (a, b) in enumerate(CANDS):
            cand_ref[c:c + 1, :] = sv_ref[2 * hd, a:a + 1, :] + sv_ref[2 * hd + 1, b:b + 1, :]
            ce_ref[c:c + 1, :] = (si_ref[2 * hd, a:a + 1, :] * float(N_KEYS)
                                  + si_ref[2 * hd + 1, b:b + 1, :])
        cand = cand_ref[...]
        ce = ce_ref[...]
        slot = lax.broadcasted_iota(I32, (N_CAND_PAD, tr), 0).astype(F32)
        for k in range(PEER_TOPK):
            m = jnp.max(cand, axis=0, keepdims=True)
            idx = jnp.min(jnp.where(cand == m, slot, float(N_CAND_PAD)), axis=0, keepdims=True)
            sel = slot == idx
            gb_ref[hd, k:k + 1, :] = m
            eb_ref[hd, k:k + 1, :] = jnp.max(jnp.where(sel, ce, -1.0), axis=0, keepdims=True)
            cand = jnp.where(sel, -jnp.inf, cand)
        best = gb_ref[hd]
        e = jnp.exp(best - best[0:1, :])
        gb_ref[hd] = e / jnp.sum(e, axis=0, keepdims=True)
        return 0

    lax.fori_loop(0, PEER_HEADS, head_body, 0, unroll=2)

    n_pick = PEER_HEADS * PEER_TOPK
    experts = eb_ref[...].reshape(n_pick, tr).T.astype(I32)
    i_out[...] = lax.shift_right_logical(experts, 7)
    j_out[...] = experts & (N_KEYS - 1)
    g_out[...] = gb_ref[...].reshape(n_pick, tr).T


def _route(hn, w, *, tr):
    t, d = hn.shape
    n_half = 2 * PEER_HEADS
    n_pick = PEER_HEADS * PEER_TOPK
    full = lambda a: pl.BlockSpec(a.shape, lambda i: (0,) * a.ndim)
    pick = pl.BlockSpec((tr, n_pick), lambda i: (i, 0))
    return pl.pallas_call(
        functools.partial(_route_kernel, tr=tr),
        grid=(t // tr,),
        in_specs=[pl.BlockSpec((tr, d), lambda i: (i, 0)), full(w["w_pq_t"]), full(w["keys"])],
        out_specs=[pick, pick, pick],
        out_shape=[jax.ShapeDtypeStruct((t, n_pick), I32), jax.ShapeDtypeStruct((t, n_pick), I32),
                   jax.ShapeDtypeStruct((t, n_pick), F32)],
        scratch_shapes=[pltpu.VMEM((n_half, N_KEYS, tr), BF16),
                        pltpu.VMEM((n_half, PEER_TOPK, tr), F32),
                        pltpu.VMEM((n_half, PEER_TOPK, tr), F32),
                        pltpu.VMEM((N_CAND_PAD, tr), F32),
                        pltpu.VMEM((N_CAND_PAD, tr), F32),
                        pltpu.VMEM((PEER_HEADS, PEER_TOPK, tr), F32),
                        pltpu.VMEM((PEER_HEADS, PEER_TOPK, tr), F32)],
        compiler_params=_params(32, 1),
        name="route",
    )(hn, w["w_pq_t"], w["keys"])


def _expert_kernel(hn_ref, x1_ref, i_ref, j_ref, g_ref, u_ref, v_ref, gfin_ref, y_ref,
                   w_ref, acc_ref, *, tt, ni, n_et):
    ne = pl.program_id(1)

    @pl.when(ne == 0)
    def _():
        acc_ref[...] = jnp.zeros_like(acc_ref)
        key_id = lax.broadcasted_iota(I32, (N_KEYS, N_KEYS), 0)
        zero = jnp.zeros((N_KEYS, N_KEYS), BF16)

        def one_hots(t):
            irow = i_ref[pl.ds(t, 1), :]
            jrow = j_ref[pl.ds(t, 1), :]
            grow = g_ref[pl.ds(t, 1), :]
            at = jnp.where(key_id == irow, grow, 0.0).astype(BF16)
            bt = jnp.where(key_id == jrow, 1.0, 0.0).astype(BF16)
            return at, bt

        def token_pair(tp, _):
            at0, bt0 = one_hots(tp)
            at1, bt1 = one_hots(tp + tt // 2)
            lhs = jnp.concatenate([at0, at1], axis=1)
            rhs = jnp.concatenate([jnp.concatenate([bt0, zero], axis=1),
                                   jnp.concatenate([zero, bt1], axis=1)], axis=0)
            w = lax.dot_general(lhs, rhs, (((1,), (1,)), ((), ())), preferred_element_type=F32)
            lo = lax.bitcast_convert_type(w[:, :N_KEYS].astype(BF16).astype(F32), U32)
            hi = lax.bitcast_convert_type(w[:, N_KEYS:].astype(BF16).astype(F32), U32)
            row0 = pl.multiple_of(tp * W_PITCH, SUBLANES)
            w_ref[pl.ds(row0, N_KEYS), :] = hi | lax.shift_right_logical(lo, jnp.uint32(16))
            return 0

        lax.fori_loop(0, tt // 2, token_pair, 0, unroll=64)

    a = lax.dot_general(hn_ref[...], u_ref[...], (((1,), (1,)), ((), ())),
                        preferred_element_type=F32)
    act = 0.5 * a * (1.0 + lax.erf(a * np.float32(np.sqrt(0.5))))

    def gate_columns(ii):
        words = w_ref[pl.ds(ne * ni + ii, tt // 2, stride=W_PITCH), :]
        first = lax.bitcast_convert_type(lax.shift_left(words, jnp.uint32(16)), F32)
        second = lax.bitcast_convert_type(words & jnp.uint32(0xFFFF0000), F32)
        return jnp.concatenate([first, second], axis=0)

    w = jnp.concatenate([gate_columns(ii) for ii in range(ni)], axis=1)
    acc_ref[...] += jnp.dot((w * act).astype(BF16), v_ref[...], preferred_element_type=F32)

    @pl.when(ne == n_et - 1)
    def _():
        y_ref[...] = _rms(x1_ref[...] + acc_ref[...], gfin_ref[...])


def _experts(hn, x1, pick_i, pick_j, pick_g, w, g_final, *, tt, ni):
    t, d = hn.shape
    n_exp = w["u"].shape[0]
    et = ni * N_KEYS
    n_et = n_exp // et
    n_pick = PEER_HEADS * PEER_TOPK
    tok = lambda width: pl.BlockSpec((tt, width), lambda i, e: (i, 0))
    return pl.pallas_call(
        functools.partial(_expert_kernel, tt=tt, ni=ni, n_et=n_et),
        grid=(t // tt, n_et),
        in_specs=[tok(d), tok(d), tok(n_pick), tok(n_pick), tok(n_pick),
                  pl.BlockSpec((et, d), lambda i, e: (e, 0)),
                  pl.BlockSpec((et, d), lambda i, e: (e, 0)),
                  pl.BlockSpec((1, d), lambda i, e: (0, 0))],
        out_specs=tok(d),
        out_shape=jax.ShapeDtypeStruct((t, d), F32),
        scratch_shapes=[pltpu.VMEM((tt // 2 * W_PITCH, N_KEYS), U32), pltpu.VMEM((tt, d), F32)],
        compiler_params=_params(56, 2),
        name="experts",
    )(hn, x1, pick_i, pick_j, pick_g, w["u"], w["v"], g_final)


def _rope_tables(pos, scale):
    freqs = ROPE_THETA ** (-jnp.arange(ROPE_HALF, dtype=F32) / ROPE_HALF)
    ang = pos.astype(F32)[:, None] * freqs[None, :]
    cos, sin = jnp.cos(ang), jnp.sin(ang)
    n = pos.shape[0]
    ones = jnp.ones((n, QK_NOPE), F32)
    tail = jnp.ones((n, HEAD_PAD - QK_NOPE - QK_ROPE), F32)
    cos_tab = jnp.concatenate([ones, cos, cos, tail], axis=1)
    sin_tab = jnp.concatenate([0 * ones, -sin, sin, 0 * tail], axis=1)
    return cos_tab * scale, sin_tab * scale, cos_tab, sin_tab


def _prep_weights(l, g_attn_norm, w_in, g_q, w_uq, g_kv, w_ukv, w_conv, g_mla_out, g_conv_out,
                  w_out, g_ffn_norm, w_peer_q, peer_keys, peer_u, peer_v):
    d = w_in.shape[1]
    zeros = lambda n: jnp.zeros((d, n), w_in.dtype)
    kpe_end = KV_RANK + Q_RANK + QK_ROPE
    w_in_r = jnp.concatenate(
        [w_in[l][:, :CKV0 + KV_RANK], zeros(ROPE_LO), w_in[l][:, CKV0 + KV_RANK:kpe_end],
         zeros(HEAD_PAD - ROPE_LO - QK_ROPE), w_in[l][:, kpe_end:]], axis=1)
    w_uq_r = jnp.pad(w_uq[l].reshape(Q_RANK, HEADS, QK_NOPE + QK_ROPE),
                     ((0, 0), (0, 0), (0, HEAD_PAD - QK_NOPE - QK_ROPE)))
    kv3 = w_ukv[l].reshape(KV_RANK, HEADS, QK_NOPE + V_DIM)
    w_k = jnp.pad(kv3[:, :, :QK_NOPE], ((0, 0), (0, 0), (0, HEAD_PAD - QK_NOPE)))
    w_vt = kv3[:, :, QK_NOPE:].reshape(KV_RANK, HEADS * V_DIM).T
    place = jnp.zeros((QK_ROPE, HEAD_PAD), F32).at[
        jnp.arange(QK_ROPE), ROPE_LO + jnp.arange(QK_ROPE)].set(1.0)
    row = lambda g: g[l].reshape(1, -1).astype(F32)
    mla_w = HEADS * V_DIM
    return {
        "w_in": w_in_r.astype(BF16), "g_attn": row(g_attn_norm), "g_q": row(g_q),
        "w_uq": w_uq_r.reshape(Q_RANK, HEADS * HEAD_PAD).astype(BF16), "g_kv": row(g_kv),
        "w_conv": w_conv[l].astype(F32), "g_conv": row(g_conv_out),
        "w_k": w_k.reshape(KV_RANK, HEADS * HEAD_PAD).astype(BF16), "w_vt": w_vt.astype(BF16),
        "place": place.astype(BF16),
        "g_mla": row(g_mla_out), "w_o_attn": w_out[l][:mla_w].astype(BF16),
        "w_o_conv": w_out[l][mla_w:].astype(BF16), "g_ffn": row(g_ffn_norm),
        "w_pq_t": w_peer_q[l].T.astype(BF16),
        "keys": peer_keys[l].reshape(2 * PEER_HEADS, N_KEYS, -1).astype(BF16),
        "u": peer_u[l].astype(BF16), "v": peer_v[l].astype(BF16),
    }


def _tile(n, pref):
    t = min(n, pref)
    assert n % t == 0, (n, pref)
    return t


def _layer(x, pos0, s_valid, past_kv, past_kpe, past_conv, w, g_final):
    b, s, d = x.shape
    scale = float(QK_NOPE + QK_ROPE) ** -0.5 * float(np.log2(np.e))
    tabs = _rope_tables(pos0 + jnp.arange(s, dtype=jnp.int32), scale)
    if past_conv is None:
        past_conv = jnp.zeros((b, 2, CONV_WIDTH), F32)
    q, ckv, kpe, convn, new_conv = _front(x, past_conv, tabs, w, ts=_tile(s, 512), s_valid=s_valid)

    if past_kv is None:
        ckv_all, kpe_all, sk_valid = ckv, kpe, s
    else:
        ckv_all = jnp.concatenate([past_kv, ckv[:, :s_valid]], axis=1)
        kpe_all = jnp.concatenate([past_kpe, kpe[:, :s_valid]], axis=1)
        sk_valid = ckv_all.shape[1]
    tk = 512 if sk_valid % 512 == 0 else 256
    pad = -sk_valid % tk
    if pad:
        ckv_all = jnp.pad(ckv_all, ((0, 0), (0, pad), (0, 0)))
        kpe_all = jnp.pad(kpe_all, ((0, 0), (0, pad), (0, 0)))
    k, vt = _kv_up(ckv_all, kpe_all, w, tk=tk)
    attn_t = _attention(q, k, vt, tq=_tile(s, 512), q_off=pos0, sk_valid=sk_valid)
    x1, hn = _merge(x, attn_t, convn, w, ts=_tile(s, 512))

    t = b * s_valid
    hn2 = hn[:, :s_valid].reshape(t, d)
    x12 = x1[:, :s_valid].reshape(t, d)
    pick_i, pick_j, pick_g = _route(hn2, w, tr=_tile(t, 1024))
    y = _experts(hn2, x12, pick_i, pick_j, pick_g, w, g_final, tt=_tile(t, 512), ni=16)
    return y.reshape(b, s_valid, d), ckv, kpe, new_conv


def kernel(x_prompt, x_sample, cache_kv_latent, cache_k_rope, state_conv, g_attn_norm, w_in, g_q, w_uq, g_kv, w_ukv, w_conv, g_mla_out, g_conv_out, w_out, g_ffn_norm, w_peer_q, peer_keys, peer_u, peer_v, g_final):
    depth = w_in.shape[0]
    assert depth == 1, "the final norm is fused into the last layer; one layer supported"
    past_len = cache_kv_latent.shape[2]
    s_dec = x_sample.shape[1]
    s_pad = -(-s_dec // LANES) * LANES
    hs = jnp.pad(x_sample, ((0, 0), (0, s_pad - s_dec), (0, 0)))
    gfin = g_final.reshape(1, -1).astype(F32)
    w = _prep_weights(0, g_attn_norm, w_in, g_q, w_uq, g_kv, w_ukv, w_conv, g_mla_out, g_conv_out,
                      w_out, g_ffn_norm, w_peer_q, peer_keys, peer_u, peer_v)
    yp, kv_p, kpe_p, conv_p = _layer(x_prompt, 0, x_prompt.shape[1], None, None, None, w, gfin)
    ys, kv_s, kpe_s, conv_s = _layer(hs, past_len, s_dec, cache_kv_latent[0], cache_k_rope[0],
                                     state_conv[0], w, gfin)
    return (yp, ys[:, :s_dec],
            kv_p[None], kpe_p[None], conv_p[None],
            kv_s[None, :, :s_dec], kpe_s[None, :, :s_dec], conv_s[None])
```
